```python
import jax, jax.numpy as jnp
from jax import lax
import numpy as np

D_MODEL = 1024
BATCH = 32
SEQ = 2048
DEPTH = 1
DEC_BATCH = 128
DEC_SEQ = 4
PAST_LEN = 8192
PAGE_SIZE = 128

MIX_WIDTH = D_MODEL
HEAD_DIM = 64
N_ATT_HEADS = (MIX_WIDTH // 2) // HEAD_DIM
ATT_WIDTH = N_ATT_HEADS * HEAD_DIM
N_ML_HEADS = 4
ML_WIDTH = MIX_WIDTH - ATT_WIDTH
ML_DK = ML_WIDTH // N_ML_HEADS
ML_DV = ML_DK
DILATED_PATTERNS = ((128, 1), (512, 4), (2048, 16))
MAX_WINDOW = 2048
ROPE_THETA = 10000.0
D_FF = 256 * ((8 * D_MODEL // 3 + 255) // 256)
ML_CHUNK = 128
NORM_EPS = 1e-6
IN_WIDTH = 3 * ATT_WIDTH + 4 * ML_WIDTH + 2 * N_ML_HEADS

kernel_name = "hymba_dilated_swa_mlstm_macaron_step"


def _rmsnorm(x, g):
    xf = x.astype(jnp.float32)
    y = xf * lax.rsqrt(jnp.mean(xf * xf, axis=-1, keepdims=True) + NORM_EPS)
    return (y * g.astype(jnp.float32)).astype(x.dtype)


def _rope(x, pos):
    half = HEAD_DIM // 2
    inv = ROPE_THETA ** (-jnp.arange(half, dtype=jnp.float32) / half)
    ang = pos.astype(jnp.float32)[:, None] * inv[None, :]
    cos = jnp.cos(ang)[None, :, None, :]
    sin = jnp.sin(ang)[None, :, None, :]
    xf = x.astype(jnp.float32)
    x1, x2 = xf[..., :half], xf[..., half:]
    return jnp.concatenate([x1 * cos - x2 * sin, x1 * sin + x2 * cos], axis=-1).astype(x.dtype)


def _swiglu(x, w_gate, w_up, w_down):
    return (jax.nn.silu(x @ w_gate) * (x @ w_up)) @ w_down


def _project(h, pos, w_in, q_gain, k_gain):
    B, T, _ = h.shape
    z = h @ w_in
    widths = (ATT_WIDTH,) * 3 + (ML_WIDTH,) * 4 + (N_ML_HEADS,) * 2
    idx = [sum(widths[:i + 1]) for i in range(len(widths) - 1)]
    aq, ak, av, mq, mk, mv, mo, mi, mf = jnp.split(z, idx, axis=-1)
    aq = _rope(_rmsnorm(aq.reshape(B, T, N_ATT_HEADS, HEAD_DIM), q_gain), pos)
    ak = _rope(_rmsnorm(ak.reshape(B, T, N_ATT_HEADS, HEAD_DIM), k_gain), pos)
    av = av.reshape(B, T, N_ATT_HEADS, HEAD_DIM)
    mq = mq.reshape(B, T, N_ML_HEADS, ML_DK)
    mk = mk.reshape(B, T, N_ML_HEADS, ML_DK) * (ML_DK ** -0.5)
    mv = mv.reshape(B, T, N_ML_HEADS, ML_DV)
    return aq, ak, av, mq, mk, mv, mo, mi, mf


def _merge_by_denominator(outs, lses):
    wts = jax.nn.softmax(jnp.stack(lses, 0), axis=0)
    return jnp.einsum('pbth,pbthd->bthd', wts, jnp.stack(outs, 0))


def _dilated_band_prompt(q, k, v, dil, n_back):
    B, S, H, Dh = q.shape
    L = S // dil
    nb = -(-L // n_back)
    Lp = nb * n_back

    def split(t):
        t = t.reshape(B, L, dil, H, Dh).transpose(0, 2, 1, 3, 4)
        t = jnp.pad(t, ((0, 0), (0, 0), (0, Lp - L), (0, 0), (0, 0)))
        return t.reshape(B, dil, nb, n_back, H, Dh)

    def with_prev(t):
        prev = jnp.pad(t[:, :, :-1], ((0, 0), (0, 0), (1, 0), (0, 0), (0, 0), (0, 0)))
        return jnp.concatenate([prev, t], axis=3)

    qb = split(q)
    kb = with_prev(split(k))
    vb = with_prev(split(v))
    s = jnp.einsum('brnqhd,brnkhd->brnhqk', qb, kb).astype(jnp.float32) * (HEAD_DIM ** -0.5)
    i = jnp.arange(n_back)[:, None]
    j = jnp.arange(2 * n_back)[None, :]
    band = (j >= i) & (j <= i + n_back)
    blk = jnp.arange(nb)[:, None, None]
    valid = band[None] & ((blk > 0) | (j[None] >= n_back))
    s = jnp.where(valid[None, None, :, None], s, -jnp.inf)
    m = jnp.max(s, axis=-1, keepdims=True)
    p = jnp.exp(s - m)
    den = jnp.sum(p, axis=-1, keepdims=True)
    o = jnp.einsum('brnhqk,brnkhd->brnqhd', p / den, vb.astype(jnp.float32))
    lse = (m + jnp.log(den))[..., 0]
    o = o.reshape(B, dil, Lp, H, Dh)[:, :, :L].transpose(0, 2, 1, 3, 4).reshape(B, S, H, Dh)
    lse = lse.transpose(0, 1, 2, 4, 3).reshape(B, dil, Lp, H)[:, :, :L]
    lse = lse.transpose(0, 2, 1, 3).reshape(B, S, H)
    return o, lse


def _dilated_attn_prompt(q, k, v):
    outs, lses = [], []
    for window, dil in DILATED_PATTERNS:
        o, l = _dilated_band_prompt(q, k, v, dil, window // dil)
        outs.append(o)
        lses.append(l)
    return _merge_by_denominator(outs, lses)


def _dilated_attn_sample(q, k_new, v_new, cache_k, cache_v):
    T = q.shape[1]
    Lb = cache_k.shape[1]
    kc = jnp.concatenate([cache_k, k_new.astype(cache_k.dtype)], axis=1)
    vc = jnp.concatenate([cache_v, v_new.astype(cache_v.dtype)], axis=1)
    t = jnp.arange(T)
    outs, lses = [], []
    for window, dil in DILATED_PATTERNS:
        n_back = window // dil
        j = jnp.arange(n_back + 1)
        rows = Lb + t[:, None] - dil * j[None, :]
        valid = rows >= 0
        rows = jnp.maximum(rows, 0)
        kg = kc[:, rows]
        vg = vc[:, rows]
        s = jnp.einsum('bthd,btjhd->bthj', q, kg).astype(jnp.float32) * (HEAD_DIM ** -0.5)
        s = jnp.where(valid[None, :, None, :], s, -jnp.inf)
        m = jnp.max(s, axis=-1, keepdims=True)
        p = jnp.exp(s - m)
        den = jnp.sum(p, axis=-1, keepdims=True)
        outs.append(jnp.einsum('bthj,btjhd->bthd', p / den, vg.astype(jnp.float32)))
        lses.append((m + jnp.log(den))[..., 0])
    return _merge_by_denominator(outs, lses)


def _mlstm_chunk(q, k, v, ig, lf, C, n, m):
    L = q.shape[1]
    b = jnp.cumsum(lf, axis=1).transpose(0, 2, 1)
    igt = ig.transpose(0, 2, 1)
    causal = jnp.tril(jnp.ones((L, L), dtype=bool))
    d = jnp.where(causal, b[..., :, None] - b[..., None, :] + igt[..., None, :], -jnp.inf)
    inter = b + m[..., None]
    m_t = jnp.maximum(jnp.max(d, axis=-1), inter)
    a = jnp.exp(d - m_t[..., None]) * jnp.einsum('bthk,bshk->bhts', q, k)
    w_inter = jnp.exp(inter - m_t)
    num = jnp.einsum('bhts,bshv->bthv', a, v) + jnp.einsum('bht,bthk,bhkv->bthv', w_inter, q, C)
    den = jnp.sum(a, axis=-1) + w_inter * jnp.einsum('bthk,bhk->bht', q, n)
    den = jnp.maximum(jnp.abs(den), jnp.exp(-m_t))
    h = num / den.transpose(0, 2, 1)[..., None]
    b_last = b[..., -1]
    g = b_last[..., None] - b + igt
    m_new = jnp.maximum(b_last + m, jnp.max(g, axis=-1))
    w_s = jnp.exp(g - m_new[..., None])
    w_c = jnp.exp(b_last + m - m_new)
    C_new = w_c[..., None, None] * C + jnp.einsum('bhs,bshk,bshv->bhkv', w_s, k, v)
    n_new = w_c[..., None] * n + jnp.einsum('bhs,bshk->bhk', w_s, k)
    return h, C_new, n_new, m_new


def _mlstm_prompt(q, k, v, ig, lf):
    B, S, H, Dk = q.shape
    ch = min(ML_CHUNK, S)
    nc = S // ch

    def to_chunks(a):
        return a.reshape((B, nc, ch) + a.shape[2:]).swapaxes(0, 1)

    C0 = jnp.zeros((B, H, Dk, ML_DV), jnp.float32)
    n0 = jnp.zeros((B, H, Dk), jnp.float32)
    m0 = jnp.zeros((B, H), jnp.float32)

    def step(carry, xs):
        C, n, m = carry
        h, C, n, m = _mlstm_chunk(*xs, C, n, m)
        return (C, n, m), h

    (C, n, m), hs = lax.scan(step, (C0, n0, m0), tuple(to_chunks(a) for a in (q, k, v, ig, lf)))
    return hs.swapaxes(0, 1).reshape(B, S, H, ML_DV), C, n, m


def _layer(x, pos, p, attn_cache, ml_state):
    B, T, _ = x.shape
    f32 = jnp.float32
    x = x + 0.5 * _swiglu(_rmsnorm(x, p['ffn1_norm']), p['ffn1_w_gate'], p['ffn1_w_up'], p['ffn1_w_down'])
    h = _rmsnorm(x, p['mix_norm'])
    aq, ak, av, mq, mk, mv, mo, mi, mf = _project(h, pos, p['w_in'], p['q_norm'], p['k_norm'])
    ig = mi.astype(f32) + p['b_igate'].astype(f32)
    lf = jax.nn.log_sigmoid(mf.astype(f32) + p['b_fgate'].astype(f32))
    mq, mk, mv = mq.astype(f32), mk.astype(f32), mv.astype(f32)
    if attn_cache is None:
        att = _dilated_attn_prompt(aq, ak, av)
        n_keep = min(MAX_WINDOW, T)
        att_state = (ak[:, T - n_keep:], av[:, T - n_keep:])
        hm, C, n, m = _mlstm_prompt(mq, mk, mv, ig, lf)
    else:
        att = _dilated_attn_sample(aq, ak, av, attn_cache[0], attn_cache[1])
        att_state = (ak, av)
        C0, n0, m0 = ml_state
        hm, C, n, m = _mlstm_chunk(mq, mk, mv, ig, lf, C0.astype(f32), n0.astype(f32), m0.astype(f32))
    hm = _rmsnorm(hm, p['ml_out_norm']) * jax.nn.sigmoid(mo.astype(f32)).reshape(B, T, N_ML_HEADS, ML_DV)
    mix = jnp.concatenate([att.reshape(B, T, ATT_WIDTH).astype(x.dtype),
                           hm.reshape(B, T, ML_WIDTH).astype(x.dtype)], axis=-1)
    x = x + mix @ p['w_out']
    x = x + 0.5 * _swiglu(_rmsnorm(x, p['ffn2_norm']), p['ffn2_w_gate'], p['ffn2_w_up'], p['ffn2_w_down'])
    return x, att_state, (C, n, m)


def setup_inputs(seed: int = 0) -> dict:
    key = jax.random.key(seed)
    ks = jax.random.split(key, 24)
    f32 = jnp.float32

    def nrm(k, shape, scale):
        return jax.random.normal(k, shape, f32) * scale

    def gain(k, shape):
        return 1.0 + 0.02 * jax.random.normal(k, shape, f32)

    win_buf = min(MAX_WINDOW, PAST_LEN)
    return {
        'x_prompt': nrm(ks[0], (BATCH, SEQ, D_MODEL), 1.0),
        'x_sample': nrm(ks[1], (DEC_BATCH, DEC_SEQ, D_MODEL), 1.0),
        'cache_k_win': nrm(ks[2], (DEPTH, DEC_BATCH, win_buf, N_ATT_HEADS, HEAD_DIM), 1.0),
        'cache_v_win': nrm(ks[3], (DEPTH, DEC_BATCH, win_buf, N_ATT_HEADS, HEAD_DIM), 1.0),
        'state_C': nrm(ks[4], (DEPTH, DEC_BATCH, N_ML_HEADS, ML_DK, ML_DV), 0.3),
        'state_n': nrm(ks[5], (DEPTH, DEC_BATCH, N_ML_HEADS, ML_DK), 0.3),
        'state_m': jax.random.uniform(ks[6], (DEPTH, DEC_BATCH, N_ML_HEADS), f32, 0.0, 4.0),
        'ffn1_norm': gain(ks[7], (DEPTH, D_MODEL)),
        'ffn1_w_gate': nrm(ks[8], (DEPTH, D_MODEL, D_FF), D_MODEL ** -0.5),
        'ffn1_w_up': nrm(ks[9], (DEPTH, D_MODEL, D_FF), D_MODEL ** -0.5),
        'ffn1_w_down': nrm(ks[10], (DEPTH, D_FF, D_MODEL), D_FF ** -0.5),
        'mix_norm': gain(ks[11], (DEPTH, D_MODEL)),
        'w_in': nrm(ks[12], (DEPTH, D_MODEL, IN_WIDTH), D_MODEL ** -0.5),
        'q_norm': gain(ks[13], (DEPTH, HEAD_DIM)),
        'k_norm': gain(ks[14], (DEPTH, HEAD_DIM)),
        'b_igate': nrm(ks[15], (DEPTH, N_ML_HEADS), 0.1),
        'b_fgate': jnp.linspace(3.0, 6.0, N_ML_HEADS, dtype=f32)[None, :] + nrm(ks[16], (DEPTH, N_ML_HEADS), 0.1),
        'ml_out_norm': gain(ks[17], (DEPTH, N_ML_HEADS, ML_DV)),
        'w_out': nrm(ks[18], (DEPTH, MIX_WIDTH, D_MODEL), MIX_WIDTH ** -0.5),
        'ffn2_norm': gain(ks[19], (DEPTH, D_MODEL)),
        'ffn2_w_gate': nrm(ks[20], (DEPTH, D_MODEL, D_FF), D_MODEL ** -0.5),
        'ffn2_w_up': nrm(ks[21], (DEPTH, D_MODEL, D_FF), D_MODEL ** -0.5),
        'ffn2_w_down': nrm(ks[22], (DEPTH, D_FF, D_MODEL), D_FF ** -0.5),
    }


def reference(x_prompt, x_sample, cache_k_win, cache_v_win, state_C, state_n, state_m,
              ffn1_norm, ffn1_w_gate, ffn1_w_up, ffn1_w_down, mix_norm, w_in, q_norm, k_norm,
              b_igate, b_fgate, ml_out_norm, w_out, ffn2_norm, ffn2_w_gate, ffn2_w_up, ffn2_w_down):
    pos_p = jnp.arange(x_prompt.shape[1], dtype=jnp.int32)
    pos_s = PAST_LEN + jnp.arange(x_sample.shape[1], dtype=jnp.int32)
    yp, ys = x_prompt, x_sample
    kp_l, vp_l, ks_l, vs_l = [], [], [], []
    Cp_l, np_l, mp_l, Cs_l, ns_l, ms_l = [], [], [], [], [], []
    for l in range(DEPTH):
        p = {
            'ffn1_norm': ffn1_norm[l], 'ffn1_w_gate': ffn1_w_gate[l], 'ffn1_w_up': ffn1_w_up[l],
            'ffn1_w_down': ffn1_w_down[l], 'mix_norm': mix_norm[l], 'w_in': w_in[l],
            'q_norm': q_norm[l], 'k_norm': k_norm[l], 'b_igate': b_igate[l], 'b_fgate': b_fgate[l],
            'ml_out_norm': ml_out_norm[l], 'w_out': w_out[l], 'ffn2_norm': ffn2_norm[l],
            'ffn2_w_gate': ffn2_w_gate[l], 'ffn2_w_up': ffn2_w_up[l], 'ffn2_w_down': ffn2_w_down[l],
        }
        yp, (kp, vp), (Cp, np_, mp) = _layer(yp, pos_p, p, None, None)
        ys, (ks, vs), (Cs, ns, ms) = _layer(ys, pos_s, p, (cache_k_win[l], cache_v_win[l]),
                                            (state_C[l], state_n[l], state_m[l]))
        kp_l.append(kp); vp_l.append(vp); ks_l.append(ks); vs_l.append(vs)
        Cp_l.append(Cp); np_l.append(np_); mp_l.append(mp)
        Cs_l.append(Cs); ns_l.append(ns); ms_l.append(ms)
    return (yp, ys, jnp.stack(kp_l), jnp.stack(vp_l), jnp.stack(ks_l), jnp.stack(vs_l),
            jnp.stack(Cp_l), jnp.stack(np_l), jnp.stack(mp_l),
            jnp.stack(Cs_l), jnp.stack(ns_l), jnp.stack(ms_l))
```

```python
import functools

import jax
import jax.numpy as jnp
from jax import lax
from jax.experimental import pallas as pl
from jax.experimental.pallas import tpu as pltpu

F32 = jnp.float32
BF16 = jnp.bfloat16

D_MODEL = 1024
HEAD_DIM = 64
N_ATT_HEADS = 8
ATT_WIDTH = N_ATT_HEADS * HEAD_DIM
N_ML_HEADS = 4
ML_DK = 128
ML_WIDTH = N_ML_HEADS * ML_DK
D_FF = 2816
ML_CHUNK = 128
NORM_EPS = 1e-6
ROPE_THETA = 10000.0
PAST_LEN = 8192
DILATED_PATTERNS = ((128, 1), (512, 4), (2048, 16))
N_BACK = 128
ATT_SCALE = HEAD_DIM ** -0.5
MAIN_WIDTH = 3 * ATT_WIDTH + 4 * ML_WIDTH

LANES = 128
N_PAIRS = ATT_WIDTH // LANES
ROW_TILE = 256
FF_CHUNK = 256
VMEM_LIMIT = 56 * 1024 * 1024
NEG_BIG = -1e30

_NT = (((1,), (1,)), ((), ()))
_TN = (((0,), (0,)), ((), ()))


def _rms(x, g):
    return x * lax.rsqrt(jnp.mean(x * x, axis=-1, keepdims=True) + NORM_EPS) * g


def _swiglu(hn, wg_ref, wu_ref, wd_ref, act_ref):
    for c in range(D_FF // FF_CHUNK):
        sl = slice(c * FF_CHUNK, (c + 1) * FF_CHUNK)
        g = jnp.dot(hn, wg_ref[:, sl], preferred_element_type=F32)
        u = jnp.dot(hn, wu_ref[:, sl], preferred_element_type=F32)
        act_ref[:, sl] = (g * jax.nn.sigmoid(g) * u).astype(BF16)
    return jnp.dot(act_ref[...], wd_ref[...], preferred_element_type=F32)


def _pair_norm_rope(x, gain, cos, sin, lo):
    sq = x * x
    s_lo = jnp.sum(jnp.where(lo, sq, 0.0), axis=-1, keepdims=True)
    s_hi = jnp.sum(jnp.where(lo, 0.0, sq), axis=-1, keepdims=True)
    ms = jnp.where(lo, s_lo, s_hi) * (1.0 / HEAD_DIM)
    y = x * lax.rsqrt(ms + NORM_EPS) * gain
    lane = lax.broadcasted_iota(jnp.int32, y.shape, 1)
    first_half = (lane % HEAD_DIM) < (HEAD_DIM // 2)
    partner = jnp.where(first_half, pltpu.roll(y, LANES - HEAD_DIM // 2, 1), pltpu.roll(y, HEAD_DIM // 2, 1))
    return y * cos + partner * sin


def _front_kernel(x_ref, cos_ref, sin_ref, g1_ref, wg_ref, wu_ref, wd_ref, gm_ref, win_ref, wgate_ref,
                  bgate_ref, qg_ref, kg_ref,
                  x1_ref, qh_ref, k_ref, v_ref, kh_ref, vh_ref, mq_ref, mk_ref, mv_ref, mo_ref, gates_ref,
                  act_ref):
    x = x_ref[...]
    hn = _rms(x, g1_ref[...]).astype(BF16)
    x1 = x + 0.5 * _swiglu(hn, wg_ref, wu_ref, wd_ref, act_ref)
    x1_ref[...] = x1
    h = _rms(x1, gm_ref[...]).astype(BF16)

    def proj(i):
        return jnp.dot(h, win_ref[:, i * ATT_WIDTH:(i + 1) * ATT_WIDTH], preferred_element_type=F32)

    lane = lax.broadcasted_iota(jnp.int32, (x.shape[0], LANES), 1)
    lo = lane < HEAD_DIM
    cos, sin = cos_ref[...], sin_ref[...]
    aq, ak, av = proj(0), proj(1), proj(2)
    v_ref[...] = av
    for p in range(N_PAIRS):
        sl = slice(p * LANES, (p + 1) * LANES)
        qh_ref[p] = _pair_norm_rope(aq[:, sl], qg_ref[...], cos, sin, lo).astype(BF16)
        kp = _pair_norm_rope(ak[:, sl], kg_ref[...], cos, sin, lo)
        k_ref[:, sl] = kp
        kh_ref[p] = kp.astype(BF16)
        vh_ref[p] = av[:, sl].astype(BF16)
    mq, mk, mv, mo = proj(3), proj(4), proj(5), proj(6)
    for p in range(N_ML_HEADS):
        sl = slice(p * ML_DK, (p + 1) * ML_DK)
        mq_ref[p] = mq[:, sl].astype(BF16)
        mk_ref[p] = (mk[:, sl] * (ML_DK ** -0.5)).astype(BF16)
        mv_ref[p] = mv[:, sl].astype(BF16)
        mo_ref[p] = mo[:, sl]
    zg = jnp.dot(h, wgate_ref[...], preferred_element_type=F32) + bgate_ref[...]
    log_sig = jnp.minimum(zg, 0.0) - jnp.log(1.0 + jnp.exp(-jnp.abs(zg)))
    gates_ref[...] = jnp.where(lane < N_ML_HEADS, zg, log_sig)


def _back_kernel(x1_ref, att_ref, hm_ref, wo_ref, g2_ref, wg_ref, wu_ref, wd_ref, y_ref, act_ref):
    x2 = (x1_ref[...]
          + jnp.dot(att_ref[...], wo_ref[:ATT_WIDTH, :], preferred_element_type=F32)
          + jnp.dot(hm_ref[...], wo_ref[ATT_WIDTH:, :], preferred_element_type=F32))
    hn = _rms(x2, g2_ref[...]).astype(BF16)
    y_ref[...] = x2 + 0.5 * _swiglu(hn, wg_ref, wu_ref, wd_ref, act_ref)


def _const_spec(shape):
    return pl.BlockSpec(shape, lambda i: (0,) * len(shape), pipeline_mode=pl.Buffered(1))


def _dense_params():
    return pltpu.CompilerParams(dimension_semantics=("arbitrary",), vmem_limit_bytes=VMEM_LIMIT)


def _front(x, cos, sin, w, rope_blocks):
    n = x.shape[0]
    tm = ROW_TILE
    row = lambda width: pl.BlockSpec((tm, width), lambda i: (i, 0))
    rope = pl.BlockSpec((tm, LANES), lambda i: (i % rope_blocks, 0))
    grouped = lambda g: pl.BlockSpec((g, tm, LANES), lambda i: (0, i, 0))
    gshape = lambda g, dt: jax.ShapeDtypeStruct((g, n, LANES), dt)
    return pl.pallas_call(
        _front_kernel,
        grid=(n // tm,),
        in_specs=[row(D_MODEL), rope, rope,
                  _const_spec((1, D_MODEL)), _const_spec((D_MODEL, D_FF)), _const_spec((D_MODEL, D_FF)),
                  _const_spec((D_FF, D_MODEL)), _const_spec((1, D_MODEL)), _const_spec((D_MODEL, MAIN_WIDTH)),
                  _const_spec((D_MODEL, LANES)), _const_spec((1, LANES)), _const_spec((1, LANES)),
                  _const_spec((1, LANES))],
        out_specs=[row(D_MODEL), grouped(N_PAIRS), row(ATT_WIDTH), row(ATT_WIDTH), grouped(N_PAIRS),
                   grouped(N_PAIRS), grouped(N_ML_HEADS), grouped(N_ML_HEADS), grouped(N_ML_HEADS),
                   grouped(N_ML_HEADS), row(LANES)],
        out_shape=[jax.ShapeDtypeStruct((n, D_MODEL), F32), gshape(N_PAIRS, BF16),
                   jax.ShapeDtypeStruct((n, ATT_WIDTH), F32), jax.ShapeDtypeStruct((n, ATT_WIDTH), F32),
                   gshape(N_PAIRS, BF16), gshape(N_PAIRS, BF16), gshape(N_ML_HEADS, BF16),
                   gshape(N_ML_HEADS, BF16), gshape(N_ML_HEADS, BF16), gshape(N_ML_HEADS, F32),
                   jax.ShapeDtypeStruct((n, LANES), F32)],
        scratch_shapes=[pltpu.VMEM((tm, D_FF), BF16)],
        compiler_params=_dense_params(),
        name="front",
    )(x, cos, sin, w["g1"], w["wg1"], w["wu1"], w["wd1"], w["gm"], w["win"], w["wgate"], w["bgate"],
      w["qg"], w["kg"])


def _back(x1, att, hm, w):
    n = x1.shape[0]
    tm = ROW_TILE
    row = lambda width: pl.BlockSpec((tm, width), lambda i: (i, 0))
    return pl.pallas_call(
        _back_kernel,
        grid=(n // tm,),
        in_specs=[row(D_MODEL), row(ATT_WIDTH), row(ML_WIDTH), _const_spec((D_MODEL, D_MODEL)),
                  _const_spec((1, D_MODEL)), _const_spec((D_MODEL, D_FF)), _const_spec((D_MODEL, D_FF)),
                  _const_spec((D_FF, D_MODEL))],
        out_specs=row(D_MODEL),
        out_shape=jax.ShapeDtypeStruct((n, D_MODEL), F32),
        scratch_shapes=[pltpu.VMEM((tm, D_FF), BF16)],
        compiler_params=_dense_params(),
        name="back",
    )(x1, att, hm, w["wo"], w["g2"], w["wg2"], w["wu2"], w["wd2"])


def _pair_attend(qb, kk, vv, mask, lo):
    outs, lses = [], []
    for sel in (lo, jnp.logical_not(lo)):
        qm = jnp.where(sel, qb, jnp.zeros_like(qb))
        s = lax.dot_general(qm, kk, _NT, preferred_element_type=F32) * ATT_SCALE
        s = jnp.where(mask, s, -jnp.inf)
        m = jnp.max(s, axis=-1, keepdims=True)
        p = jnp.exp(s - m)
        den = jnp.sum(p, axis=-1, keepdims=True)
        outs.append(jnp.dot(p.astype(BF16), vv, preferred_element_type=F32) / den)
        lses.append(m + jnp.log(den))
    shape = outs[0].shape
    return (jnp.where(lo, outs[0], outs[1]),
            jnp.where(lo, jnp.broadcast_to(lses[0], shape), jnp.broadcast_to(lses[1], shape)))


def _attn_prompt_kernel(q1, k1, v1, q4, k4, v4, q16, k16, v16, o_ref, o4_s, l4_s, o16_s, l16_s):
    nb = N_BACK
    lo = lax.broadcasted_iota(jnp.int32, (nb, LANES), 1) < HEAD_DIM
    r2 = lax.broadcasted_iota(jnp.int32, (nb, 2 * nb), 0)
    c2 = lax.broadcasted_iota(jnp.int32, (nb, 2 * nb), 1)
    band = (c2 >= r2) & (c2 <= r2 + nb)
    causal = (lax.broadcasted_iota(jnp.int32, (nb, nb), 1)
              <= lax.broadcasted_iota(jnp.int32, (nb, nb), 0))

    def block(q, k, v, n, cols):
        if isinstance(n, int) and n == 0:
            rows = slice(0, nb)
            return _pair_attend(q[rows, cols], k[rows, cols], v[rows, cols], causal, lo)
        qrows = pl.ds(pl.multiple_of(n * nb, nb), nb)
        krows = pl.ds(pl.multiple_of((n - 1) * nb, nb), 2 * nb)
        return _pair_attend(q[qrows, cols], k[krows, cols], v[krows, cols], band, lo)

    d16 = DILATED_PATTERNS[2][1]
    for r in range(d16):
        o, l = block(q16, k16, v16, 0, slice(r * LANES, (r + 1) * LANES))
        o16_s[pl.ds(r, nb, stride=d16), :] = o
        l16_s[pl.ds(r, nb, stride=d16), :] = l

    d4 = DILATED_PATTERNS[1][1]
    nblk4 = q4.shape[0] // nb
    for r in range(d4):
        cols = slice(r * LANES, (r + 1) * LANES)

        def store4(n, ol):
            o4_s[n, pl.ds(r, nb, stride=d4), :] = ol[0]
            l4_s[n, pl.ds(r, nb, stride=d4), :] = ol[1]

        store4(0, block(q4, k4, v4, 0, cols))

        def body4(n, carry):
            store4(n, block(q4, k4, v4, n, cols))
            return carry

        lax.fori_loop(1, nblk4, body4, 0)

    span = nb * d4

    def merge_store(n, ol):
        o1, l1 = ol
        rows = pl.ds(pl.multiple_of(n * nb, nb), nb) if not isinstance(n, int) else slice(n * nb, (n + 1) * nb)
        blk4 = n // (span // nb)
        off4 = (n % (span // nb)) * nb
        if isinstance(n, int):
            rows4 = slice(off4, off4 + nb)
        else:
            rows4 = pl.ds(pl.multiple_of(off4, nb), nb)
        o4, l4 = o4_s[blk4, rows4, :], l4_s[blk4, rows4, :]
        o16, l16 = o16_s[rows, :], l16_s[rows, :]
        m = jnp.maximum(jnp.maximum(l1, l4), l16)
        w1, w4, w16 = jnp.exp(l1 - m), jnp.exp(l4 - m), jnp.exp(l16 - m)
        o_ref[rows, :] = ((w1 * o1 + w4 * o4 + w16 * o16) / (w1 + w4 + w16)).astype(o_ref.dtype)

    all_cols = slice(0, LANES)
    merge_store(0, block(q1, k1, v1, 0, all_cols))

    def body1(n, carry):
        merge_store(n, block(q1, k1, v1, n, all_cols))
        return carry

    lax.fori_loop(1, q1.shape[0] // nb, body1, 0)


def _attn_prompt(qh, kh, vh, batch, seq):
    specs, args = [], []
    for _, dil in DILATED_PATTERNS:
        rows = seq // dil
        for t in (qh, kh, vh):
            args.append(t.reshape(N_PAIRS, batch, rows, dil * LANES))
            specs.append(pl.BlockSpec((None, None, rows, dil * LANES), lambda b, p: (p, b, 0, 0)))
    d4 = DILATED_PATTERNS[1][1]
    span = N_BACK * d4
    return pl.pallas_call(
        _attn_prompt_kernel,
        grid=(batch, N_PAIRS),
        in_specs=specs,
        out_specs=pl.BlockSpec((None, seq, LANES), lambda b, p: (b, 0, p)),
        out_shape=jax.ShapeDtypeStruct((batch, seq, ATT_WIDTH), BF16),
        scratch_shapes=[pltpu.VMEM((seq // span, span, LANES), F32), pltpu.VMEM((seq // span, span, LANES), F32),
                        pltpu.VMEM((seq, LANES), F32), pltpu.VMEM((seq, LANES), F32)],
        compiler_params=pltpu.CompilerParams(dimension_semantics=("arbitrary", "arbitrary"),
                                             vmem_limit_bytes=VMEM_LIMIT),
        name="attn_prompt",
    )(*args)


def _attn_sample_kernel(q_ref, kn_ref, vn_ref, ka_ref, va_ref, kb_ref, vb_ref, o_ref):
    T = q_ref.shape[0]
    na = ka_ref.shape[0]
    d4, d16 = DILATED_PATTERNS[1][1], DILATED_PATTERNS[2][1]
    q = q_ref[...]
    kn, vn = kn_ref[...], vn_ref[...]
    lane8 = lax.broadcasted_iota(jnp.int32, (N_ATT_HEADS, ATT_WIDTH), 1)
    head8 = lax.broadcasted_iota(jnp.int32, (N_ATT_HEADS, ATT_WIDTH), 0)
    own = (lane8 // HEAD_DIM) == head8
    qrows = jnp.concatenate(
        [jnp.where(own, jnp.broadcast_to(q[t:t + 1, :], own.shape), 0.0) for t in range(T)], axis=0)
    nr = qrows.shape[0]
    qb = qrows.astype(BF16)
    trow = lax.broadcasted_iota(jnp.int32, (nr, 1), 0) // N_ATT_HEADS
    ka, va = ka_ref[...].astype(BF16), va_ref[...].astype(BF16)
    s_a = lax.dot_general(qb, ka, _NT, preferred_element_type=F32) * ATT_SCALE
    knb = kn.astype(BF16).astype(F32)
    s_new = [jnp.sum(qb.astype(F32) * knb[t:t + 1, :], axis=-1, keepdims=True) * ATT_SCALE for t in range(T)]
    col = lax.broadcasted_iota(jnp.int32, (nr, na), 1)
    trow_b = jnp.broadcast_to(trow, (nr, na))
    vnb = vn.astype(BF16).astype(F32)

    def finish(s_cache, p_dot, new_ok):
        m = jnp.max(s_cache, axis=-1, keepdims=True)
        for t in range(T):
            m = jnp.maximum(m, jnp.where(new_ok(t), s_new[t], -jnp.inf))
        p = jnp.exp(s_cache - m)
        den = jnp.sum(p, axis=-1, keepdims=True)
        acc = p_dot(p.astype(BF16))
        for t in range(T):
            pt = jnp.where(new_ok(t), jnp.exp(s_new[t] - m), 0.0)
            den = den + pt
            acc = acc + pt.astype(BF16).astype(F32) * vnb[t:t + 1, :]
        return acc / den, m + jnp.log(den)

    dot_a = lambda p: jnp.dot(p, va, preferred_element_type=F32)
    o1, l1 = finish(jnp.where(col >= na - N_BACK + trow_b, s_a, -jnp.inf), dot_a, lambda t: trow >= t)
    o4, l4 = finish(jnp.where((col % d4) == trow_b, s_a, -jnp.inf), dot_a, lambda t: trow == t)
    o16s, l16s = [], []
    for t in range(T):
        rows = slice(t * N_ATT_HEADS, (t + 1) * N_ATT_HEADS)
        cols = slice(t * ATT_WIDTH, (t + 1) * ATT_WIDTH)
        kb, vb = kb_ref[:, cols].astype(BF16), vb_ref[:, cols].astype(BF16)
        s_b = lax.dot_general(qb[rows], kb, _NT, preferred_element_type=F32) * ATT_SCALE
        m = jnp.maximum(jnp.max(s_b, axis=-1, keepdims=True), s_new[t][rows])
        p = jnp.exp(s_b - m)
        pt = jnp.exp(s_new[t][rows] - m)
        den = jnp.sum(p, axis=-1, keepdims=True) + pt
        acc = jnp.dot(p.astype(BF16), vb, preferred_element_type=F32) + pt.astype(BF16).astype(F32) * vnb[t:t + 1, :]
        o16s.append(acc / den)
        l16s.append(m + jnp.log(den))
    o16, l16 = jnp.concatenate(o16s, axis=0), jnp.concatenate(l16s, axis=0)
    m = jnp.maximum(jnp.maximum(l1, l4), l16)
    w1, w4, w16 = jnp.exp(l1 - m), jnp.exp(l4 - m), jnp.exp(l16 - m)
    merged = (w1 * o1 + w4 * o4 + w16 * o16) / (w1 + w4 + w16)
    for t in range(T):
        rows = slice(t * N_ATT_HEADS, (t + 1) * N_ATT_HEADS)
        o_ref[t:t + 1, :] = jnp.sum(jnp.where(own, merged[rows], 0.0), axis=0, keepdims=True).astype(o_ref.dtype)


def _attn_sample(q, k_new, v_new, cache_k, cache_v):
    bd, T, _ = q.shape
    lb = cache_k.shape[1]
    (_, _), (w4, d4), (w16, d16) = DILATED_PATTERNS
    assert lb >= w16 and lb % w4 == 0 and lb % d16 == 0 and T <= d4
    tail_blocks = lb // w4
    new_spec = pl.BlockSpec((None, T, ATT_WIDTH), lambda b: (b, 0, 0))
    tail_spec = pl.BlockSpec((None, w4, ATT_WIDTH), lambda b: (b, tail_blocks - 1, 0))
    strided_spec = pl.BlockSpec((None, lb // d16, T * ATT_WIDTH), lambda b: (b, 0, 0))
    strided = lambda c: c.reshape(bd, lb // d16, d16 * ATT_WIDTH)
    return pl.pallas_call(
        _attn_sample_kernel,
        grid=(bd,),
        in_specs=[new_spec, new_spec, new_spec, tail_spec, tail_spec, strided_spec, strided_spec],
        out_specs=new_spec,
        out_shape=jax.ShapeDtypeStruct((bd, T, ATT_WIDTH), BF16),
        compiler_params=pltpu.CompilerParams(dimension_semantics=("arbitrary",), vmem_limit_bytes=VMEM_LIMIT),
        name="attn_sample",
    )(q, k_new, v_new, cache_k, cache_v, strided(cache_k), strided(cache_v))


def _mlstm_kernel(q_ref, k_ref, v_ref, mo_ref, g_ref, gain_ref, c0_ref, n0_ref, m0_ref,
                  hm_ref, c_ref, n_ref, m_ref, *, chunk):
    L = chunk
    head = pl.program_id(1)
    c_ref[...] = c0_ref[...]
    n_ref[...] = n0_ref[...]
    m_ref[...] = m0_ref[...]
    row = lax.broadcasted_iota(jnp.int32, (L, L), 0)
    col = lax.broadcasted_iota(jnp.int32, (L, L), 1)
    tril = col <= row
    tril_f = tril.astype(F32)
    lane = lax.broadcasted_iota(jnp.int32, (1, LANES), 1)
    sel_ig = (lane == head).astype(F32)
    sel_lf = (lane == head + N_ML_HEADS).astype(F32)
    pick_row = lax.broadcasted_iota(jnp.int32, (8, LANES), 0)
    pick = jnp.where(pick_row == 0, jnp.broadcast_to(sel_ig, (8, LANES)),
                     jnp.where(pick_row == 1, jnp.broadcast_to(sel_lf, (8, LANES)), 0.0))
    gain = gain_ref[...]
    hi = lax.Precision.HIGHEST

    def body(c, carry):
        rows = pl.ds(pl.multiple_of(c * L, L), L)
        g = g_ref[rows, :]
        b_all = jnp.dot(tril_f, g, preferred_element_type=F32, precision=hi)
        bcol = jnp.sum(b_all * sel_lf, axis=-1, keepdims=True)
        igcol = jnp.sum(g * sel_ig, axis=-1, keepdims=True)
        brow = lax.dot_general(pick, b_all, _NT, preferred_element_type=F32, precision=hi)[1:2, :]
        igrow = lax.dot_general(pick, g, _NT, preferred_element_type=F32, precision=hi)[0:1, :]
        q, k, v = q_ref[rows, :], k_ref[rows, :], v_ref[rows, :]
        m_prev = m_ref[...][:, 0:1]
        cmat, nvec = c_ref[...], n_ref[...]
        d = jnp.where(tril, bcol - brow + igrow, -jnp.inf)
        inter = bcol + m_prev
        m_t = jnp.maximum(jnp.max(d, axis=-1, keepdims=True), inter)
        a = jnp.exp(d - m_t) * lax.dot_general(q, k, _NT, preferred_element_type=F32)
        w_inter = jnp.exp(inter - m_t)
        num = (jnp.dot(a.astype(BF16), v, preferred_element_type=F32)
               + w_inter * jnp.dot(q, cmat.astype(BF16), preferred_element_type=F32))
        den = jnp.sum(a, axis=-1, keepdims=True) + w_inter * jnp.sum(q.astype(F32) * nvec, axis=-1, keepdims=True)
        den = jnp.maximum(jnp.abs(den), jnp.exp(-m_t))
        hh = num / den
        y = _rms(hh, gain) * jax.nn.sigmoid(mo_ref[rows, :])
        hm_ref[rows, :] = y.astype(hm_ref.dtype)
        b_last = bcol[L - 1:L, :]
        gcol = b_last - bcol + igcol
        m_new = jnp.maximum(b_last + m_prev, jnp.max(gcol, axis=0, keepdims=True))
        w_s = jnp.exp(gcol - m_new)
        w_c = jnp.exp(b_last + m_prev - m_new)
        c_ref[...] = w_c * cmat + lax.dot_general(k, (w_s * v.astype(F32)).astype(BF16), _TN,
                                                  preferred_element_type=F32)
        n_ref[...] = w_c * nvec + jnp.sum(w_s * k.astype(F32), axis=0, keepdims=True)
        m_ref[...] = jnp.broadcast_to(m_new, m_ref.shape)
        return carry

    lax.fori_loop(0, q_ref.shape[0] // L, body, 0)


def _mlstm(mq, mk, mv, mo, gates, gain, c0, n0, m0, chunk):
    _, batch, seq, _ = mq.shape
    per_head = pl.BlockSpec((None, None, seq, LANES), lambda b, h: (h, b, 0, 0))
    state = lambda r: pl.BlockSpec((None, None, r, LANES), lambda b, h: (b, h, 0, 0))
    return pl.pallas_call(
        functools.partial(_mlstm_kernel, chunk=chunk),
        grid=(batch, N_ML_HEADS),
        in_specs=[per_head, per_head, per_head, per_head,
                  pl.BlockSpec((None, seq, LANES), lambda b, h: (b, 0, 0)),
                  pl.BlockSpec((None, 1, LANES), lambda b, h: (h, 0, 0)),
                  state(ML_DK), state(1), state(1)],
        out_specs=[pl.BlockSpec((None, seq, LANES), lambda b, h: (b, 0, h)), state(ML_DK), state(1), state(1)],
        out_shape=[jax.ShapeDtypeStruct((batch, seq, ML_WIDTH), BF16),
                   jax.ShapeDtypeStruct((batch, N_ML_HEADS, ML_DK, LANES), F32),
                   jax.ShapeDtypeStruct((batch, N_ML_HEADS, 1, LANES), F32),
                   jax.ShapeDtypeStruct((batch, N_ML_HEADS, 1, LANES), F32)],
        compiler_params=pltpu.CompilerParams(dimension_semantics=("arbitrary", "arbitrary"),
                                             vmem_limit_bytes=VMEM_LIMIT),
        name="mlstm",
    )(mq, mk, mv, mo, gates, gain, c0, n0, m0)


def _rope_tables(pos):
    half = HEAD_DIM // 2
    inv = ROPE_THETA ** (-jnp.arange(half, dtype=F32) / half)
    ang = pos.astype(F32)[:, None] * inv[None, :]
    cos, sin = jnp.cos(ang), jnp.sin(ang)
    return jnp.tile(jnp.concatenate([cos, cos], axis=-1), (1, 2)), jnp.tile(jnp.concatenate([-sin, sin], axis=-1), (1, 2))


def _layer_weights(l, ffn1_norm, ffn1_w_gate, ffn1_w_up, ffn1_w_down, mix_norm, w_in, q_norm, k_norm, b_igate,
                   b_fgate, ml_out_norm, w_out, ffn2_norm, ffn2_w_gate, ffn2_w_up, ffn2_w_down):
    pad_lanes = lambda a: jnp.pad(a, ((0, 0), (0, LANES - a.shape[1])))
    return {
        "g1": ffn1_norm[l][None, :], "wg1": ffn1_w_gate[l].astype(BF16), "wu1": ffn1_w_up[l].astype(BF16),
        "wd1": ffn1_w_down[l].astype(BF16), "gm": mix_norm[l][None, :],
        "win": w_in[l][:, :MAIN_WIDTH].astype(BF16), "wgate": pad_lanes(w_in[l][:, MAIN_WIDTH:]).astype(BF16),
        "bgate": pad_lanes(jnp.concatenate([b_igate[l], b_fgate[l]])[None, :]),
        "qg": jnp.tile(q_norm[l], 2)[None, :], "kg": jnp.tile(k_norm[l], 2)[None, :],
        "gain": ml_out_norm[l][:, None, :], "wo": w_out[l].astype(BF16),
        "g2": ffn2_norm[l][None, :], "wg2": ffn2_w_gate[l].astype(BF16), "wu2": ffn2_w_up[l].astype(BF16),
        "wd2": ffn2_w_down[l].astype(BF16),
    }


def _prompt_layer(x, w):
    batch, seq, _ = x.shape
    n = batch * seq
    cos, sin = _rope_tables(jnp.arange(seq, dtype=jnp.int32))
    x1, qh, k, v, kh, vh, mq, mk, mv, mo, gates = _front(x.reshape(n, D_MODEL), cos, sin, w, seq // ROW_TILE)
    att = _attn_prompt(qh, kh, vh, batch, seq)
    heads = lambda t: t.reshape(N_ML_HEADS, batch, seq, LANES)
    zeros = lambda r: jnp.zeros((batch, N_ML_HEADS, r, LANES), F32)
    hm, c, nv, m = _mlstm(heads(mq), heads(mk), heads(mv), heads(mo), gates.reshape(batch, seq, LANES), w["gain"],
                          zeros(ML_DK), zeros(1), zeros(1), min(ML_CHUNK, seq))
    y = _back(x1, att.reshape(n, ATT_WIDTH), hm.reshape(n, ML_WIDTH), w)
    kv = lambda t: t.reshape(batch, seq, N_ATT_HEADS, HEAD_DIM)
    return y.reshape(batch, seq, D_MODEL), kv(k), kv(v), c, nv[:, :, 0, :], m[:, :, 0, 0]


def _sample_layer(x, w, cache_k, cache_v, c0, n0, m0):
    bd, T, _ = x.shape
    n = bd * T
    lb = cache_k.shape[1]
    cos, sin = _rope_tables(PAST_LEN + jnp.arange(T, dtype=jnp.int32))
    cos, sin = jnp.tile(cos, (bd, 1)), jnp.tile(sin, (bd, 1))
    x1, qh, k, v, _, _, mq, mk, mv, mo, gates = _front(x.reshape(n, D_MODEL), cos, sin, w, n // ROW_TILE)
    q = qh.astype(F32).transpose(1, 0, 2).reshape(bd, T, ATT_WIDTH)
    att = _attn_sample(q, k.reshape(bd, T, ATT_WIDTH), v.reshape(bd, T, ATT_WIDTH),
                       cache_k.reshape(bd, lb, ATT_WIDTH), cache_v.reshape(bd, lb, ATT_WIDTH))
    tp = 8
    heads = lambda t: jnp.pad(t.reshape(N_ML_HEADS, bd, T, LANES), ((0, 0), (0, 0), (0, tp - T), (0, 0)))
    pad_gate = jnp.where(jnp.arange(LANES) < N_ML_HEADS, NEG_BIG, 0.0).astype(F32)
    gates = jnp.concatenate([gates.reshape(bd, T, LANES), jnp.broadcast_to(pad_gate, (bd, tp - T, LANES))], axis=1)
    hm, c, nv, m = _mlstm(heads(mq), heads(mk), heads(mv), heads(mo), gates, w["gain"], c0, n0[:, :, None, :],
                          jnp.broadcast_to(m0[:, :, None, None], (bd, N_ML_HEADS, 1, LANES)), tp)
    y = _back(x1, att.reshape(n, ATT_WIDTH), hm[:, :T].reshape(n, ML_WIDTH), w)
    kv = lambda t: t.reshape(bd, T, N_ATT_HEADS, HEAD_DIM)
    return y.reshape(bd, T, D_MODEL), kv(k), kv(v), c, nv[:, :, 0, :], m[:, :, 0, 0]


def kernel(x_prompt, x_sample, cache_k_win, cache_v_win, state_C, state_n, state_m, ffn1_norm, ffn1_w_gate,
           ffn1_w_up, ffn1_w_down, mix_norm, w_in, q_norm, k_norm, b_igate, b_fgate, ml_out_norm, w_out,
           ffn2_norm, ffn2_w_gate, ffn2_w_up, ffn2_w_down):
    depth = w_in.shape[0]
    yp, ys = x_prompt, x_sample
    outs = [[] for _ in range(10)]
    for l in range(depth):
        w = _layer_weights(l, ffn1_norm, ffn1_w_gate, ffn1_w_up, ffn1_w_down, mix_norm, w_in, q_norm, k_norm,
                           b_igate, b_fgate, ml_out_norm, w_out, ffn2_norm, ffn2_w_gate, ffn2_w_up, ffn2_w_down)
        yp, kp, vp, cp, npr, mp = _prompt_layer(yp, w)
        bd, lb = cache_k_win.shape[1], cache_k_win.shape[2]
        ys, ks, vs, cs, ns, ms = _sample_layer(ys, w, cache_k_win[l].reshape(bd, lb, ATT_WIDTH),
                                               cache_v_win[l].reshape(bd, lb, ATT_WIDTH),
                                               state_C[l], state_n[l], state_m[l])
        for acc, val in zip(outs, (kp, vp, ks, vs, cp, npr, mp, cs, ns, ms)):
            acc.append(val)
    return (yp, ys) + tuple(jnp.stack(o) for o in outs)
```

```python
import functools

import jax
import jax.numpy as jnp
from jax import lax
from jax.experimental import pallas as pl
from jax.experimental.pallas import tpu as pltpu

F32 = jnp.float32
BF16 = jnp.bfloat16

D_MODEL = 1024
HEAD_DIM = 64
N_ATT_HEADS = 8
ATT_WIDTH = N_ATT_HEADS * HEAD_DIM
N_ML_HEADS = 4
ML_DK = 128
ML_WIDTH = N_ML_HEADS * ML_DK
D_FF = 2816
ML_CHUNK = 128
NORM_EPS = 1e-6
ROPE_THETA = 10000.0
PAST_LEN = 8192
DILATED_PATTERNS = ((128, 1), (512, 4), (2048, 16))
N_BACK = 128
ATT_SCALE = HEAD_DIM ** -0.5
MAIN_WIDTH = 3 * ATT_WIDTH + 4 * ML_WIDTH

LANES = 128
N_PAIRS = ATT_WIDTH // LANES
ROW_TILE = 256
FF_CHUNK = 256
ATT_GROUP = 4
VMEM_LIMIT = 56 * 1024 * 1024
NEG_BIG = -1e30

_NT = (((1,), (1,)), ((), ()))
_TN = (((0,), (0,)), ((), ()))


def _rms(x, g):
    return x * lax.rsqrt(jnp.mean(x * x, axis=-1, keepdims=True) + NORM_EPS) * g


def _swiglu(hn, wg_ref, wu_ref, wd_ref, act_ref):
    for c in range(D_FF // FF_CHUNK):
        sl = slice(c * FF_CHUNK, (c + 1) * FF_CHUNK)
        g = jnp.dot(hn, wg_ref[:, sl], preferred_element_type=F32)
        u = jnp.dot(hn, wu_ref[:, sl], preferred_element_type=F32)
        act_ref[:, sl] = (g * jax.nn.sigmoid(g) * u).astype(BF16)
    return jnp.dot(act_ref[...], wd_ref[...], preferred_element_type=F32)


def _pair_norm_rope(x, gain, cos, sin, lo):
    sq = x * x
    s_lo = jnp.sum(jnp.where(lo, sq, 0.0), axis=-1, keepdims=True)
    s_hi = jnp.sum(jnp.where(lo, 0.0, sq), axis=-1, keepdims=True)
    ms = jnp.where(lo, s_lo, s_hi) * (1.0 / HEAD_DIM)
    y = x * lax.rsqrt(ms + NORM_EPS) * gain
    lane = lax.broadcasted_iota(jnp.int32, y.shape, 1)
    first_half = (lane % HEAD_DIM) < (HEAD_DIM // 2)
    partner = jnp.where(first_half, pltpu.roll(y, LANES - HEAD_DIM // 2, 1), pltpu.roll(y, HEAD_DIM // 2, 1))
    return y * cos + partner * sin


def _front_kernel(x_ref, cos_ref, sin_ref, g1_ref, wg_ref, wu_ref, wd_ref, gm_ref, win_ref, wgate_ref,
                  bgate_ref, qg_ref, kg_ref,
                  x1_ref, q_ref, k_ref, v_ref, mq_ref, mk_ref, mv_ref, mo_ref, gates_ref,
                  act_ref):
    x = x_ref[...]
    hn = _rms(x, g1_ref[...]).astype(BF16)
    x1 = x + 0.5 * _swiglu(hn, wg_ref, wu_ref, wd_ref, act_ref)
    x1_ref[...] = x1
    h = _rms(x1, gm_ref[...]).astype(BF16)

    def proj(i):
        return jnp.dot(h, win_ref[:, i * ATT_WIDTH:(i + 1) * ATT_WIDTH], preferred_element_type=F32)

    lane = lax.broadcasted_iota(jnp.int32, (x.shape[0], LANES), 1)
    lo = lane < HEAD_DIM
    cos, sin = cos_ref[...], sin_ref[...]
    aq, ak = proj(0), proj(1)
    v_ref[...] = proj(2)
    for p in range(N_PAIRS):
        sl = slice(p * LANES, (p + 1) * LANES)
        q_ref[:, sl] = _pair_norm_rope(aq[:, sl], qg_ref[...], cos, sin, lo) * ATT_SCALE
        k_ref[:, sl] = _pair_norm_rope(ak[:, sl], kg_ref[...], cos, sin, lo)
    mq, mk, mv, mo = proj(3), proj(4), proj(5), proj(6)
    for p in range(N_ML_HEADS):
        sl = slice(p * ML_DK, (p + 1) * ML_DK)
        mq_ref[p] = mq[:, sl].astype(BF16)
        mk_ref[p] = (mk[:, sl] * (ML_DK ** -0.5)).astype(BF16)
        mv_ref[p] = mv[:, sl].astype(BF16)
        mo_ref[p] = mo[:, sl]
    zg = jnp.dot(h, wgate_ref[...], preferred_element_type=F32) + bgate_ref[...]
    log_sig = jnp.minimum(zg, 0.0) - jnp.log(1.0 + jnp.exp(-jnp.abs(zg)))
    gates_ref[...] = jnp.where(lane < N_ML_HEADS, zg, log_sig)


def _back_kernel(x1_ref, att_ref, hm_ref, wo_ref, g2_ref, wg_ref, wu_ref, wd_ref, y_ref, act_ref):
    x2 = (x1_ref[...]
          + jnp.dot(att_ref[...], wo_ref[:ATT_WIDTH, :], preferred_element_type=F32)
          + jnp.dot(hm_ref[...], wo_ref[ATT_WIDTH:, :], preferred_element_type=F32))
    hn = _rms(x2, g2_ref[...]).astype(BF16)
    y_ref[...] = x2 + 0.5 * _swiglu(hn, wg_ref, wu_ref, wd_ref, act_ref)


def _const_spec(shape):
    return pl.BlockSpec(shape, lambda i: (0,) * len(shape), pipeline_mode=pl.Buffered(1))


def _dense_params():
    return pltpu.CompilerParams(dimension_semantics=("arbitrary",), vmem_limit_bytes=VMEM_LIMIT)


def _front(x, cos, sin, w, rope_blocks):
    n = x.shape[0]
    tm = ROW_TILE
    row = lambda width: pl.BlockSpec((tm, width), lambda i: (i, 0))
    rope = pl.BlockSpec((tm, LANES), lambda i: (i % rope_blocks, 0))
    grouped = lambda g: pl.BlockSpec((g, tm, LANES), lambda i: (0, i, 0))
    gshape = lambda g, dt: jax.ShapeDtypeStruct((g, n, LANES), dt)
    return pl.pallas_call(
        _front_kernel,
        grid=(n // tm,),
        in_specs=[row(D_MODEL), rope, rope,
                  _const_spec((1, D_MODEL)), _const_spec((D_MODEL, D_FF)), _const_spec((D_MODEL, D_FF)),
                  _const_spec((D_FF, D_MODEL)), _const_spec((1, D_MODEL)), _const_spec((D_MODEL, MAIN_WIDTH)),
                  _const_spec((D_MODEL, LANES)), _const_spec((1, LANES)), _const_spec((1, LANES)),
                  _const_spec((1, LANES))],
        out_specs=[row(D_MODEL), row(ATT_WIDTH), row(ATT_WIDTH), row(ATT_WIDTH), grouped(N_ML_HEADS),
                   grouped(N_ML_HEADS), grouped(N_ML_HEADS), grouped(N_ML_HEADS), row(LANES)],
        out_shape=[jax.ShapeDtypeStruct((n, D_MODEL), F32), jax.ShapeDtypeStruct((n, ATT_WIDTH), F32),
                   jax.ShapeDtypeStruct((n, ATT_WIDTH), F32), jax.ShapeDtypeStruct((n, ATT_WIDTH), F32),
                   gshape(N_ML_HEADS, BF16), gshape(N_ML_HEADS, BF16), gshape(N_ML_HEADS, BF16),
                   gshape(N_ML_HEADS, F32), jax.ShapeDtypeStruct((n, LANES), F32)],
        scratch_shapes=[pltpu.VMEM((tm, D_FF), BF16)],
        compiler_params=_dense_params(),
        name="front",
    )(x, cos, sin, w["g1"], w["wg1"], w["wu1"], w["wd1"], w["gm"], w["win"], w["wgate"], w["bgate"],
      w["qg"], w["kg"])


def _back(x1, att, hm, w):
    n = x1.shape[0]
    tm = ROW_TILE
    row = lambda width: pl.BlockSpec((tm, width), lambda i: (i, 0))
    return pl.pallas_call(
        _back_kernel,
        grid=(n // tm,),
        in_specs=[row(D_MODEL), row(ATT_WIDTH), row(ML_WIDTH), _const_spec((D_MODEL, D_MODEL)),
                  _const_spec((1, D_MODEL)), _const_spec((D_MODEL, D_FF)), _const_spec((D_MODEL, D_FF)),
                  _const_spec((D_FF, D_MODEL))],
        out_specs=row(D_MODEL),
        out_shape=jax.ShapeDtypeStruct((n, D_MODEL), F32),
        scratch_shapes=[pltpu.VMEM((tm, D_FF), BF16)],
        compiler_params=_dense_params(),
        name="back",
    )(x1, att, hm, w["wo"], w["g2"], w["wg2"], w["wu2"], w["wd2"])


def _attend_group(jobs, lo):
    zero = jnp.zeros((), BF16)
    one = jnp.ones((), BF16)
    scores, values = [], []
    for q, k, v, _ in jobs:
        lo_k = lax.broadcasted_iota(jnp.int32, k.shape, 1) < HEAD_DIM
        k2 = jnp.concatenate([jnp.where(lo_k, k, zero), jnp.where(lo_k, zero, k)], axis=0)
        scores.append(lax.dot_general(q, k2, _NT, preferred_element_type=F32))
        ones_lo = lo_k.astype(F32).astype(BF16)
        ones_hi = one - ones_lo
        values.append(jnp.concatenate(
            [jnp.concatenate([jnp.where(lo_k, v, zero), ones_lo], axis=1),
             jnp.concatenate([jnp.where(lo_k, zero, v), ones_hi], axis=1)], axis=0))
    masked = []
    for (_, k, _, mask), s in zip(jobs, scores):
        nk = k.shape[0]
        masked.append([jnp.where(mask, s[:, h * nk:(h + 1) * nk], -jnp.inf) for h in range(2)])
    maxes = [[jnp.max(s, axis=-1, keepdims=True) for s in pair] for pair in masked]
    probs = [jnp.concatenate([jnp.exp(s - m).astype(BF16) for s, m in zip(pair, mx)], axis=1)
             for pair, mx in zip(masked, maxes)]
    sums = [jnp.dot(p, v2, preferred_element_type=F32) for p, v2 in zip(probs, values)]
    return [(r[:, :LANES], jnp.where(lo, mx[0], mx[1]), r[:, LANES:]) for r, mx in zip(sums, maxes)]


def _attn_prompt_kernel(q_ref, k_ref, v_ref, o_ref, acc4_s, m4_s, den4_s, acc16_s, m16_s, den16_s):
    nb = N_BACK
    seq = q_ref.shape[0]
    lo = lax.broadcasted_iota(jnp.int32, (nb, LANES), 1) < HEAD_DIM
    r2 = lax.broadcasted_iota(jnp.int32, (nb, 2 * nb), 0)
    c2 = lax.broadcasted_iota(jnp.int32, (nb, 2 * nb), 1)
    band = (c2 >= r2) & (c2 <= r2 + nb)
    causal = (lax.broadcasted_iota(jnp.int32, (nb, nb), 1)
              <= lax.broadcasted_iota(jnp.int32, (nb, nb), 0))

    def job(dil, r, n):
        rows = lambda blk: pl.ds(r + blk * nb * dil, nb, stride=dil) if dil > 1 else pl.ds(blk * nb, nb)
        load = lambda ref, blk: ref[rows(blk), :].astype(BF16)
        q = load(q_ref, n)
        if n == 0:
            return q, load(k_ref, 0), load(v_ref, 0), causal
        return (q, jnp.concatenate([load(k_ref, n - 1), load(k_ref, n)], axis=0),
                jnp.concatenate([load(v_ref, n - 1), load(v_ref, n)], axis=0), band)

    def dilated(dil, acc_s, m_s, den_s):
        blocks = [(r, n) for n in range(seq // (nb * dil)) for r in range(dil)]
        for g in range(0, len(blocks), ATT_GROUP):
            group = blocks[g:g + ATT_GROUP]
            for (r, n), (acc, m, den) in zip(group, _attend_group([job(dil, r, n) for r, n in group], lo)):
                rows = pl.ds(r + n * nb * dil, nb, stride=dil)
                acc_s[rows, :] = acc
                m_s[rows, :] = m
                den_s[rows, :] = den

    dilated(DILATED_PATTERNS[2][1], acc16_s, m16_s, den16_s)
    dilated(DILATED_PATTERNS[1][1], acc4_s, m4_s, den4_s)

    for g in range(0, seq // nb, ATT_GROUP):
        group = list(range(g, g + ATT_GROUP))
        for n, (acc1, m1, den1) in zip(group, _attend_group([job(1, 0, n) for n in group], lo)):
            rows = pl.ds(n * nb, nb)
            m4, m16 = m4_s[rows, :], m16_s[rows, :]
            m = jnp.maximum(jnp.maximum(m1, m4), m16)
            w1, w4, w16 = jnp.exp(m1 - m), jnp.exp(m4 - m), jnp.exp(m16 - m)
            num = w1 * acc1 + w4 * acc4_s[rows, :] + w16 * acc16_s[rows, :]
            den = w1 * den1 + w4 * den4_s[rows, :] + w16 * den16_s[rows, :]
            o_ref[rows, :] = (num / den).astype(o_ref.dtype)


def _attn_prompt(q, k, v, batch, seq):
    for window, dil in DILATED_PATTERNS:
        assert window // dil == N_BACK and seq % (N_BACK * dil) == 0
    pair = pl.BlockSpec((None, seq, LANES), lambda b, p: (b, 0, p))
    return pl.pallas_call(
        _attn_prompt_kernel,
        grid=(batch, N_PAIRS),
        in_specs=[pair, pair, pair],
        out_specs=pair,
        out_shape=jax.ShapeDtypeStruct((batch, seq, ATT_WIDTH), BF16),
        scratch_shapes=[pltpu.VMEM((seq, LANES), F32)] * 6,
        compiler_params=pltpu.CompilerParams(dimension_semantics=("arbitrary", "arbitrary"),
                                             vmem_limit_bytes=VMEM_LIMIT),
        name="attn_prompt",
    )(q, k, v)


def _attn_sample_kernel(q_ref, kn_ref, vn_ref, ka_ref, va_ref, kb_ref, vb_ref, o_ref):
    T = q_ref.shape[0]
    na = ka_ref.shape[0]
    d4 = DILATED_PATTERNS[1][1]
    q = q_ref[...]
    kn, vn = kn_ref[...], vn_ref[...]
    lane8 = lax.broadcasted_iota(jnp.int32, (N_ATT_HEADS, ATT_WIDTH), 1)
    head8 = lax.broadcasted_iota(jnp.int32, (N_ATT_HEADS, ATT_WIDTH), 0)
    own = (lane8 // HEAD_DIM) == head8
    qrows = jnp.concatenate(
        [jnp.where(own, jnp.broadcast_to(q[t:t + 1, :], own.shape), 0.0) for t in range(T)], axis=0)
    nr = qrows.shape[0]
    qb = qrows.astype(BF16)
    trow = lax.broadcasted_iota(jnp.int32, (nr, 1), 0) // N_ATT_HEADS
    ka, va = ka_ref[...].astype(BF16), va_ref[...].astype(BF16)
    s_a = lax.dot_general(qb, ka, _NT, preferred_element_type=F32)
    knb = kn.astype(BF16).astype(F32)
    s_new = [jnp.sum(qb.astype(F32) * knb[t:t + 1, :], axis=-1, keepdims=True) for t in range(T)]
    col = lax.broadcasted_iota(jnp.int32, (nr, na), 1)
    trow_b = jnp.broadcast_to(trow, (nr, na))
    vnb = vn.astype(BF16).astype(F32)

    def finish(s_cache, p_dot, new_ok):
        m = jnp.max(s_cache, axis=-1, keepdims=True)
        for t in range(T):
            m = jnp.maximum(m, jnp.where(new_ok(t), s_new[t], -jnp.inf))
        p = jnp.exp(s_cache - m)
        den = jnp.sum(p, axis=-1, keepdims=True)
        acc = p_dot(p.astype(BF16))
        for t in range(T):
            pt = jnp.where(new_ok(t), jnp.exp(s_new[t] - m), 0.0)
            den = den + pt
            acc = acc + pt.astype(BF16).astype(F32) * vnb[t:t + 1, :]
        return acc / den, m + jnp.log(den)

    dot_a = lambda p: jnp.dot(p, va, preferred_element_type=F32)
    o1, l1 = finish(jnp.where(col >= na - N_BACK + trow_b, s_a, -jnp.inf), dot_a, lambda t: trow >= t)
    o4, l4 = finish(jnp.where((col % d4) == trow_b, s_a, -jnp.inf), dot_a, lambda t: trow == t)
    o16s, l16s = [], []
    for t in range(T):
        rows = slice(t * N_ATT_HEADS, (t + 1) * N_ATT_HEADS)
        cols = slice(t * ATT_WIDTH, (t + 1) * ATT_WIDTH)
        kb, vb = kb_ref[:, cols].astype(BF16), vb_ref[:, cols].astype(BF16)
        s_b = lax.dot_general(qb[rows], kb, _NT, preferred_element_type=F32)
        m = jnp.maximum(jnp.max(s_b, axis=-1, keepdims=True), s_new[t][rows])
        p = jnp.exp(s_b - m)
        pt = jnp.exp(s_new[t][rows] - m)
        den = jnp.sum(p, axis=-1, keepdims=True) + pt
        acc = jnp.dot(p.astype(BF16), vb, preferred_element_type=F32) + pt.astype(BF16).astype(F32) * vnb[t:t + 1, :]
        o16s.append(acc / den)
        l16s.append(m + jnp.log(den))
    o16, l16 = jnp.concatenate(o16s, axis=0), jnp.concatenate(l16s, axis=0)
    m = jnp.maximum(jnp.maximum(l1, l4), l16)
    w1, w4, w16 = jnp.exp(l1 - m), jnp.exp(l4 - m), jnp.exp(l16 - m)
    merged = (w1 * o1 + w4 * o4 + w16 * o16) / (w1 + w4 + w16)
    for t in range(T):
        rows = slice(t * N_ATT_HEADS, (t + 1) * N_ATT_HEADS)
        o_ref[t:t + 1, :] = jnp.sum(jnp.where(own, merged[rows], 0.0), axis=0, keepdims=True).astype(o_ref.dtype)


def _attn_sample(q, k_new, v_new, cache_k, cache_v):
    bd, T, _ = q.shape
    lb = cache_k.shape[1]
    (_, _), (w4, d4), (w16, d16) = DILATED_PATTERNS
    assert lb >= w16 and lb % w4 == 0 and lb % d16 == 0 and T <= d4
    tail_blocks = lb // w4
    new_spec = pl.BlockSpec((None, T, ATT_WIDTH), lambda b: (b, 0, 0))
    tail_spec = pl.BlockSpec((None, w4, ATT_WIDTH), lambda b: (b, tail_blocks - 1, 0))
    strided_spec = pl.BlockSpec((None, lb // d16, T * ATT_WIDTH), lambda b: (b, 0, 0))
    strided = lambda c: c.reshape(bd, lb // d16, d16 * ATT_WIDTH)
    return pl.pallas_call(
        _attn_sample_kernel,
        grid=(bd,),
        in_specs=[new_spec, new_spec, new_spec, tail_spec, tail_spec, strided_spec, strided_spec],
        out_specs=new_spec,
        out_shape=jax.ShapeDtypeStruct((bd, T, ATT_WIDTH), BF16),
        compiler_params=pltpu.CompilerParams(dimension_semantics=("arbitrary",), vmem_limit_bytes=VMEM_LIMIT),
        name="attn_sample",
    )(q, k_new, v_new, cache_k, cache_v, strided(cache_k), strided(cache_v))


def _mlstm_kernel(q_ref, k_ref, v_ref, mo_ref, g_ref, gain_ref, c0_ref, n0_ref, m0_ref,
                  hm_ref, c_ref, n_ref, m_ref, *, chunk):
    L = chunk
    heads = range(N_ML_HEADS)
    c_ref[...] = c0_ref[...]
    n_ref[...] = n0_ref[...]
    m_ref[...] = m0_ref[...]
    row = lax.broadcasted_iota(jnp.int32, (L, L), 0)
    col = lax.broadcasted_iota(jnp.int32, (L, L), 1)
    tril = col <= row
    tril_f = tril.astype(F32)
    eye8 = (lax.broadcasted_iota(jnp.int32, (8, LANES), 0) == lax.broadcasted_iota(jnp.int32, (8, LANES), 1)).astype(F32)
    hi = lax.Precision.HIGHEST

    def body(c, carry):
        rows = pl.ds(pl.multiple_of(c * L, L), L)
        g = g_ref[rows, :]
        b_all = jnp.dot(tril_f, g, preferred_element_type=F32, precision=hi)
        g_rows = lax.dot_general(eye8, g, _NT, preferred_element_type=F32, precision=hi)
        b_rows = lax.dot_general(eye8, b_all, _NT, preferred_element_type=F32, precision=hi)
        q = [q_ref[h, rows, :] for h in heads]
        k = [k_ref[h, rows, :] for h in heads]
        v = [v_ref[h, rows, :] for h in heads]
        bcol = [b_all[:, N_ML_HEADS + h:N_ML_HEADS + h + 1] for h in heads]
        igcol = [g[:, h:h + 1] for h in heads]
        d = [jnp.where(tril, bcol[h] - b_rows[N_ML_HEADS + h:N_ML_HEADS + h + 1, :] + g_rows[h:h + 1, :], -jnp.inf)
             for h in heads]
        m_in = [jnp.max(d[h], axis=-1, keepdims=True) for h in heads]
        qk = [lax.dot_general(q[h], k[h], _NT, preferred_element_type=F32) for h in heads]
        a = [jnp.exp(d[h] - m_in[h]) * qk[h] for h in heads]
        num_in = [jnp.dot(a[h].astype(BF16), v[h], preferred_element_type=F32) for h in heads]
        den_in = [jnp.sum(a[h], axis=-1, keepdims=True) for h in heads]
        b_last = [bcol[h][L - 1:L, :] for h in heads]
        gcol = [b_last[h] - bcol[h] + igcol[h] for h in heads]
        g_max = [jnp.max(gcol[h], axis=0, keepdims=True) for h in heads]
        w_s = [jnp.exp(gcol[h] - g_max[h]) for h in heads]
        kv = [lax.dot_general(k[h], (w_s[h] * v[h].astype(F32)).astype(BF16), _TN, preferred_element_type=F32)
              for h in heads]
        k_sum = [jnp.sum(w_s[h] * k[h].astype(F32), axis=0, keepdims=True) for h in heads]
        m_prev = [m_ref[h][:, 0:1] for h in heads]
        cmat = [c_ref[h] for h in heads]
        nvec = [n_ref[h] for h in heads]
        q_c = [jnp.dot(q[h], cmat[h].astype(BF16), preferred_element_type=F32) for h in heads]
        q_n = [jnp.sum(q[h].astype(F32) * nvec[h], axis=-1, keepdims=True) for h in heads]
        for h in heads:
            inter = bcol[h] + m_prev[h]
            m_t = jnp.maximum(m_in[h], inter)
            f_in, f_st = jnp.exp(m_in[h] - m_t), jnp.exp(inter - m_t)
            num = f_in * num_in[h] + f_st * q_c[h]
            den = f_in * den_in[h] + f_st * q_n[h]
            den = jnp.maximum(jnp.abs(den), jnp.exp(-m_t))
            y = _rms(num / den, gain_ref[h]) * jax.nn.sigmoid(mo_ref[h, rows, :])
            hm_ref[rows, h * ML_DK:(h + 1) * ML_DK] = y.astype(hm_ref.dtype)
            m_new = jnp.maximum(b_last[h] + m_prev[h], g_max[h])
            w_c = jnp.exp(b_last[h] + m_prev[h] - m_new)
            w_k = jnp.exp(g_max[h] - m_new)
            c_ref[h] = w_c * cmat[h] + w_k * kv[h]
            n_ref[h] = w_c * nvec[h] + w_k * k_sum[h]
            m_ref[h] = jnp.broadcast_to(m_new, (1, LANES))
        return carry

    lax.fori_loop(0, q_ref.shape[1] // L, body, 0)


def _mlstm(mq, mk, mv, mo, gates, gain, c0, n0, m0, chunk):
    nh, batch, seq, _ = mq.shape
    per_head = pl.BlockSpec((nh, None, seq, LANES), lambda b: (0, b, 0, 0))
    state = lambda r: pl.BlockSpec((None, nh, r, LANES), lambda b: (b, 0, 0, 0))
    return pl.pallas_call(
        functools.partial(_mlstm_kernel, chunk=chunk),
        grid=(batch,),
        in_specs=[per_head, per_head, per_head, per_head,
                  pl.BlockSpec((None, seq, LANES), lambda b: (b, 0, 0)),
                  pl.BlockSpec((nh, 1, LANES), lambda b: (0, 0, 0)),
                  state(ML_DK), state(1), state(1)],
        out_specs=[pl.BlockSpec((None, seq, ML_WIDTH), lambda b: (b, 0, 0)), state(ML_DK), state(1), state(1)],
        out_shape=[jax.ShapeDtypeStruct((batch, seq, ML_WIDTH), BF16),
                   jax.ShapeDtypeStruct((batch, nh, ML_DK, LANES), F32),
                   jax.ShapeDtypeStruct((batch, nh, 1, LANES), F32),
                   jax.ShapeDtypeStruct((batch, nh, 1, LANES), F32)],
        compiler_params=pltpu.CompilerParams(dimension_semantics=("arbitrary",), vmem_limit_bytes=VMEM_LIMIT),
        name="mlstm",
    )(mq, mk, mv, mo, gates, gain, c0, n0, m0)


def _rope_tables(pos):
    half = HEAD_DIM // 2
    inv = ROPE_THETA ** (-jnp.arange(half, dtype=F32) / half)
    ang = pos.astype(F32)[:, None] * inv[None, :]
    cos, sin = jnp.cos(ang), jnp.sin(ang)
    return jnp.tile(jnp.concatenate([cos, cos], axis=-1), (1, 2)), jnp.tile(jnp.concatenate([-sin, sin], axis=-1), (1, 2))


def _layer_weights(l, ffn1_norm, ffn1_w_gate, ffn1_w_up, ffn1_w_down, mix_norm, w_in, q_norm, k_norm, b_igate,
                   b_fgate, ml_out_norm, w_out, ffn2_norm, ffn2_w_gate, ffn2_w_up, ffn2_w_down):
    pad_lanes = lambda a: jnp.pad(a, ((0, 0), (0, LANES - a.shape[1])))
    return {
        "g1": ffn1_norm[l][None, :], "wg1": ffn1_w_gate[l].astype(BF16), "wu1": ffn1_w_up[l].astype(BF16),
        "wd1": ffn1_w_down[l].astype(BF16), "gm": mix_norm[l][None, :],
        "win": w_in[l][:, :MAIN_WIDTH].astype(BF16), "wgate": pad_lanes(w_in[l][:, MAIN_WIDTH:]).astype(BF16),
        "bgate": pad_lanes(jnp.concatenate([b_igate[l], b_fgate[l]])[None, :]),
        "qg": jnp.tile(q_norm[l], 2)[None, :], "kg": jnp.tile(k_norm[l], 2)[None, :],
        "gain": ml_out_norm[l][:, None, :], "wo": w_out[l].astype(BF16),
        "g2": ffn2_norm[l][None, :], "wg2": ffn2_w_gate[l].astype(BF16), "wu2": ffn2_w_up[l].astype(BF16),
        "wd2": ffn2_w_down[l].astype(BF16),
    }


def _prompt_layer(x, w):
    batch, seq, _ = x.shape
    n = batch * seq
    cos, sin = _rope_tables(jnp.arange(seq, dtype=jnp.int32))
    x1, q, k, v, mq, mk, mv, mo, gates = _front(x.reshape(n, D_MODEL), cos, sin, w, seq // ROW_TILE)
    seqs = lambda t: t.reshape(batch, seq, ATT_WIDTH)
    att = _attn_prompt(seqs(q), seqs(k), seqs(v), batch, seq)
    heads = lambda t: t.reshape(N_ML_HEADS, batch, seq, LANES)
    zeros = lambda r: jnp.zeros((batch, N_ML_HEADS, r, LANES), F32)
    hm, c, nv, m = _mlstm(heads(mq), heads(mk), heads(mv), heads(mo), gates.reshape(batch, seq, LANES), w["gain"],
                          zeros(ML_DK), zeros(1), zeros(1), min(ML_CHUNK, seq))
    y = _back(x1, att.reshape(n, ATT_WIDTH), hm.reshape(n, ML_WIDTH), w)
    kv = lambda t: t.reshape(batch, seq, N_ATT_HEADS, HEAD_DIM)
    return y.reshape(batch, seq, D_MODEL), kv(k), kv(v), c, nv[:, :, 0, :], m[:, :, 0, 0]


def _sample_layer(x, w, cache_k, cache_v, c0, n0, m0):
    bd, T, _ = x.shape
    n = bd * T
    cos, sin = _rope_tables(PAST_LEN + jnp.arange(T, dtype=jnp.int32))
    cos, sin = jnp.tile(cos, (bd, 1)), jnp.tile(sin, (bd, 1))
    x1, q, k, v, mq, mk, mv, mo, gates = _front(x.reshape(n, D_MODEL), cos, sin, w, n // ROW_TILE)
    seqs = lambda t: t.reshape(bd, T, ATT_WIDTH)
    att = _attn_sample(seqs(q), seqs(k), seqs(v), cache_k, cache_v)
    tp = 8
    heads = lambda t: jnp.pad(t.reshape(N_ML_HEADS, bd, T, LANES), ((0, 0), (0, 0), (0, tp - T), (0, 0)))
    pad_gate = jnp.where(jnp.arange(LANES) < N_ML_HEADS, NEG_BIG, 0.0).astype(F32)
    gates = jnp.concatenate([gates.reshape(bd, T, LANES), jnp.broadcast_to(pad_gate, (bd, tp - T, LANES))], axis=1)
    hm, c, nv, m = _mlstm(heads(mq), heads(mk), heads(mv), heads(mo), gates, w["gain"], c0, n0[:, :, None, :],
                          jnp.broadcast_to(m0[:, :, None, None], (bd, N_ML_HEADS, 1, LANES)), tp)
    y = _back(x1, att.reshape(n, ATT_WIDTH), hm[:, :T].reshape(n, ML_WIDTH), w)
    kv = lambda t: t.reshape(bd, T, N_ATT_HEADS, HEAD_DIM)
    return y.reshape(bd, T, D_MODEL), kv(k), kv(v), c, nv[:, :, 0, :], m[:, :, 0, 0]


def kernel(x_prompt, x_sample, cache_k_win, cache_v_win, state_C, state_n, state_m, ffn1_norm, ffn1_w_gate,
           ffn1_w_up, ffn1_w_down, mix_norm, w_in, q_norm, k_norm, b_igate, b_fgate, ml_out_norm, w_out,
           ffn2_norm, ffn2_w_gate, ffn2_w_up, ffn2_w_down):
    depth = w_in.shape[0]
    yp, ys = x_prompt, x_sample
    outs = [[] for _ in range(10)]
    for l in range(depth):
        w = _layer_weights(l, ffn1_norm, ffn1_w_gate, ffn1_w_up, ffn1_w_down, mix_norm, w_in, q_norm, k_norm,
                           b_igate, b_fgate, ml_out_norm, w_out, ffn2_norm, ffn2_w_gate, ffn2_w_up, ffn2_w_down)
        yp, kp, vp, cp, npr, mp = _prompt_layer(yp, w)
        bd, lb = cache_k_win.shape[1], cache_k_win.shape[2]
        ys, ks, vs, cs, ns, ms = _sample_layer(ys, w, cache_k_win[l].reshape(bd, lb, ATT_WIDTH),
                                               cache_v_win[l].reshape(bd, lb, ATT_WIDTH),
                                               state_C[l], state_n[l], state_m[l])
        for acc, val in zip(outs, (kp, vp, ks, vs, cp, npr, mp, cs, ns, ms)):
            acc.append(val)
    return (yp, ys) + tuple(jnp.stack(o) for o in outs)
```

```python
import functools

import jax
import jax.numpy as jnp
from jax import lax
from jax.experimental import pallas as pl
from jax.experimental.pallas import tpu as pltpu

F32 = jnp.float32
BF16 = jnp.bfloat16

D_MODEL = 1024
HEAD_DIM = 64
N_ATT_HEADS = 8
ATT_WIDTH = N_ATT_HEADS * HEAD_DIM
N_ML_HEADS = 4
ML_DK = 128
ML_WIDTH = N_ML_HEADS * ML_DK
D_FF = 2816
ML_CHUNK = 128
NORM_EPS = 1e-6
ROPE_THETA = 10000.0
PAST_LEN = 8192
DILATED_PATTERNS = ((128, 1), (512, 4), (2048, 16))
N_BACK = 128
ATT_SCALE = HEAD_DIM ** -0.5
MAIN_WIDTH = 3 * ATT_WIDTH + 4 * ML_WIDTH

LANES = 128
N_PAIRS = ATT_WIDTH // LANES
ROW_TILE = 256
FF_CHUNK = 256
ATT_GROUP = 4
VMEM_LIMIT = 56 * 1024 * 1024
NEG_BIG = -1e30

_NT = (((1,), (1,)), ((), ()))
_TN = (((0,), (0,)), ((), ()))


def _rms(x, g):
    return x * lax.rsqrt(jnp.mean(x * x, axis=-1, keepdims=True) + NORM_EPS) * g


def _swiglu(hn, wg_ref, wu_ref, wd_ref, act_ref):
    for c in range(D_FF // FF_CHUNK):
        sl = slice(c * FF_CHUNK, (c + 1) * FF_CHUNK)
        g = jnp.dot(hn, wg_ref[:, sl], preferred_element_type=F32)
        u = jnp.dot(hn, wu_ref[:, sl], preferred_element_type=F32)
        act_ref[:, sl] = (g * jax.nn.sigmoid(g) * u).astype(BF16)
    return jnp.dot(act_ref[...], wd_ref[...], preferred_element_type=F32)


def _pair_norm_rope(x, gain, cos, sin, lo):
    sq = x * x
    s_lo = jnp.sum(jnp.where(lo, sq, 0.0), axis=-1, keepdims=True)
    s_hi = jnp.sum(jnp.where(lo, 0.0, sq), axis=-1, keepdims=True)
    ms = jnp.where(lo, s_lo, s_hi) * (1.0 / HEAD_DIM)
    y = x * lax.rsqrt(ms + NORM_EPS) * gain
    lane = lax.broadcasted_iota(jnp.int32, y.shape, 1)
    first_half = (lane % HEAD_DIM) < (HEAD_DIM // 2)
    partner = jnp.where(first_half, pltpu.roll(y, LANES - HEAD_DIM // 2, 1), pltpu.roll(y, HEAD_DIM // 2, 1))
    return y * cos + partner * sin


def _front_kernel(x_ref, cos_ref, sin_ref, g1_ref, wg_ref, wu_ref, wd_ref, gm_ref, win_ref, wgate_ref,
                  bgate_ref, qg_ref, kg_ref,
                  x1_ref, q_ref, k_ref, v_ref, kt_ref, vt_ref, mq_ref, mk_ref, mv_ref, mo_ref, gates_ref,
                  act_ref):
    x = x_ref[...]
    hn = _rms(x, g1_ref[...]).astype(BF16)
    x1 = x + 0.5 * _swiglu(hn, wg_ref, wu_ref, wd_ref, act_ref)
    x1_ref[...] = x1
    h = _rms(x1, gm_ref[...]).astype(BF16)

    def proj(i):
        return jnp.dot(h, win_ref[:, i * ATT_WIDTH:(i + 1) * ATT_WIDTH], preferred_element_type=F32)

    lane = lax.broadcasted_iota(jnp.int32, (x.shape[0], LANES), 1)
    lo = lane < HEAD_DIM
    cos, sin = cos_ref[...], sin_ref[...]
    aq, ak, av = proj(0), proj(1), proj(2)
    pairs = [slice(p * LANES, (p + 1) * LANES) for p in range(N_PAIRS)]
    for sl in pairs:
        q_ref[:, sl] = _pair_norm_rope(aq[:, sl], qg_ref[...], cos, sin, lo) * ATT_SCALE
    ak = jnp.concatenate([_pair_norm_rope(ak[:, sl], kg_ref[...], cos, sin, lo) for sl in pairs], axis=1)
    k_ref[...] = ak
    v_ref[...] = av
    kt_ref[...] = ak.T
    vt_ref[...] = av.T
    mq, mk, mv, mo = proj(3), proj(4), proj(5), proj(6)
    for p in range(N_ML_HEADS):
        sl = slice(p * ML_DK, (p + 1) * ML_DK)
        mq_ref[p] = mq[:, sl].astype(BF16)
        mk_ref[p] = (mk[:, sl] * (ML_DK ** -0.5)).astype(BF16)
        mv_ref[p] = mv[:, sl].astype(BF16)
        mo_ref[p] = mo[:, sl]
    zg = jnp.dot(h, wgate_ref[...], preferred_element_type=F32) + bgate_ref[...]
    log_sig = jnp.minimum(zg, 0.0) - jnp.log(1.0 + jnp.exp(-jnp.abs(zg)))
    gates_ref[...] = jnp.where(lane < N_ML_HEADS, zg, log_sig)


def _back_kernel(x1_ref, att_ref, hm_ref, wo_ref, g2_ref, wg_ref, wu_ref, wd_ref, y_ref, act_ref):
    x2 = (x1_ref[...]
          + jnp.dot(att_ref[...], wo_ref[:ATT_WIDTH, :], preferred_element_type=F32)
          + jnp.dot(hm_ref[...], wo_ref[ATT_WIDTH:, :], preferred_element_type=F32))
    hn = _rms(x2, g2_ref[...]).astype(BF16)
    y_ref[...] = x2 + 0.5 * _swiglu(hn, wg_ref, wu_ref, wd_ref, act_ref)


def _const_spec(shape):
    return pl.BlockSpec(shape, lambda i: (0,) * len(shape), pipeline_mode=pl.Buffered(1))


def _dense_params():
    return pltpu.CompilerParams(dimension_semantics=("arbitrary",), vmem_limit_bytes=VMEM_LIMIT)


def _front(x, cos, sin, w, seq):
    n = x.shape[0]
    tm = ROW_TILE
    blocks = seq // tm
    row = lambda width: pl.BlockSpec((tm, width), lambda i: (i, 0))
    rope = pl.BlockSpec((tm, LANES), lambda i: (i % blocks, 0))
    grouped = lambda g: pl.BlockSpec((g, tm, LANES), lambda i: (0, i, 0))
    gshape = lambda g, dt: jax.ShapeDtypeStruct((g, n, LANES), dt)
    transposed = pl.BlockSpec((None, ATT_WIDTH, tm), lambda i: (i // blocks, 0, i % blocks))
    tshape = jax.ShapeDtypeStruct((n // seq, ATT_WIDTH, seq), F32)
    return pl.pallas_call(
        _front_kernel,
        grid=(n // tm,),
        in_specs=[row(D_MODEL), rope, rope,
                  _const_spec((1, D_MODEL)), _const_spec((D_MODEL, D_FF)), _const_spec((D_MODEL, D_FF)),
                  _const_spec((D_FF, D_MODEL)), _const_spec((1, D_MODEL)), _const_spec((D_MODEL, MAIN_WIDTH)),
                  _const_spec((D_MODEL, LANES)), _const_spec((1, LANES)), _const_spec((1, LANES)),
                  _const_spec((1, LANES))],
        out_specs=[row(D_MODEL), row(ATT_WIDTH), row(ATT_WIDTH), row(ATT_WIDTH), transposed, transposed,
                   grouped(N_ML_HEADS), grouped(N_ML_HEADS), grouped(N_ML_HEADS), grouped(N_ML_HEADS), row(LANES)],
        out_shape=[jax.ShapeDtypeStruct((n, D_MODEL), F32), jax.ShapeDtypeStruct((n, ATT_WIDTH), F32),
                   jax.ShapeDtypeStruct((n, ATT_WIDTH), F32), jax.ShapeDtypeStruct((n, ATT_WIDTH), F32),
                   tshape, tshape, gshape(N_ML_HEADS, BF16), gshape(N_ML_HEADS, BF16), gshape(N_ML_HEADS, BF16),
                   gshape(N_ML_HEADS, F32), jax.ShapeDtypeStruct((n, LANES), F32)],
        scratch_shapes=[pltpu.VMEM((tm, D_FF), BF16)],
        compiler_params=_dense_params(),
        name="front",
    )(x, cos, sin, w["g1"], w["wg1"], w["wu1"], w["wd1"], w["gm"], w["win"], w["wgate"], w["bgate"],
      w["qg"], w["kg"])


def _back(x1, att, hm, w):
    n = x1.shape[0]
    tm = ROW_TILE
    row = lambda width: pl.BlockSpec((tm, width), lambda i: (i, 0))
    return pl.pallas_call(
        _back_kernel,
        grid=(n // tm,),
        in_specs=[row(D_MODEL), row(ATT_WIDTH), row(ML_WIDTH), _const_spec((D_MODEL, D_MODEL)),
                  _const_spec((1, D_MODEL)), _const_spec((D_MODEL, D_FF)), _const_spec((D_MODEL, D_FF)),
                  _const_spec((D_FF, D_MODEL))],
        out_specs=row(D_MODEL),
        out_shape=jax.ShapeDtypeStruct((n, D_MODEL), F32),
        scratch_shapes=[pltpu.VMEM((tm, D_FF), BF16)],
        compiler_params=_dense_params(),
        name="back",
    )(x1, att, hm, w["wo"], w["g2"], w["wg2"], w["wu2"], w["wd2"])


def _attend_group(jobs, lo):
    zero = jnp.zeros((), BF16)
    one = jnp.ones((), BF16)
    scores, values = [], []
    for q, k, v, _ in jobs:
        lo_k = lax.broadcasted_iota(jnp.int32, k.shape, 1) < HEAD_DIM
        k2 = jnp.concatenate([jnp.where(lo_k, k, zero), jnp.where(lo_k, zero, k)], axis=0)
        scores.append(lax.dot_general(q, k2, _NT, preferred_element_type=F32))
        ones_lo = lo_k.astype(F32).astype(BF16)
        ones_hi = one - ones_lo
        values.append(jnp.concatenate(
            [jnp.concatenate([jnp.where(lo_k, v, zero), ones_lo], axis=1),
             jnp.concatenate([jnp.where(lo_k, zero, v), ones_hi], axis=1)], axis=0))
    masked = []
    for (_, k, _, mask), s in zip(jobs, scores):
        nk = k.shape[0]
        masked.append([jnp.where(mask, s[:, h * nk:(h + 1) * nk], -jnp.inf) for h in range(2)])
    maxes = [[jnp.max(s, axis=-1, keepdims=True) for s in pair] for pair in masked]
    probs = [jnp.concatenate([jnp.exp(s - m).astype(BF16) for s, m in zip(pair, mx)], axis=1)
             for pair, mx in zip(masked, maxes)]
    sums = [jnp.dot(p, v2, preferred_element_type=F32) for p, v2 in zip(probs, values)]
    return [(r[:, :LANES], jnp.where(lo, mx[0], mx[1]), r[:, LANES:]) for r, mx in zip(sums, maxes)]


def _attn_prompt_kernel(q_ref, k_ref, v_ref, o_ref, acc4_s, m4_s, den4_s, acc16_s, m16_s, den16_s):
    nb = N_BACK
    seq = q_ref.shape[0]
    lo = lax.broadcasted_iota(jnp.int32, (nb, LANES), 1) < HEAD_DIM
    r2 = lax.broadcasted_iota(jnp.int32, (nb, 2 * nb), 0)
    c2 = lax.broadcasted_iota(jnp.int32, (nb, 2 * nb), 1)
    band = (c2 >= r2) & (c2 <= r2 + nb)
    causal = (lax.broadcasted_iota(jnp.int32, (nb, nb), 1)
              <= lax.broadcasted_iota(jnp.int32, (nb, nb), 0))

    def job(dil, r, n):
        rows = lambda blk: pl.ds(r + blk * nb * dil, nb, stride=dil) if dil > 1 else pl.ds(blk * nb, nb)
        load = lambda ref, blk: ref[rows(blk), :].astype(BF16)
        q = load(q_ref, n)
        if n == 0:
            return q, load(k_ref, 0), load(v_ref, 0), causal
        return (q, jnp.concatenate([load(k_ref, n - 1), load(k_ref, n)], axis=0),
                jnp.concatenate([load(v_ref, n - 1), load(v_ref, n)], axis=0), band)

    def dilated(dil, acc_s, m_s, den_s):
        blocks = [(r, n) for n in range(seq // (nb * dil)) for r in range(dil)]
        for g in range(0, len(blocks), ATT_GROUP):
            group = blocks[g:g + ATT_GROUP]
            for (r, n), (acc, m, den) in zip(group, _attend_group([job(dil, r, n) for r, n in group], lo)):
                rows = pl.ds(r + n * nb * dil, nb, stride=dil)
                acc_s[rows, :] = acc
                m_s[rows, :] = m
                den_s[rows, :] = den

    dilated(DILATED_PATTERNS[2][1], acc16_s, m16_s, den16_s)
    dilated(DILATED_PATTERNS[1][1], acc4_s, m4_s, den4_s)

    for g in range(0, seq // nb, ATT_GROUP):
        group = list(range(g, g + ATT_GROUP))
        for n, (acc1, m1, den1) in zip(group, _attend_group([job(1, 0, n) for n in group], lo)):
            rows = pl.ds(n * nb, nb)
            m4, m16 = m4_s[rows, :], m16_s[rows, :]
            m = jnp.maximum(jnp.maximum(m1, m4), m16)
            w1, w4, w16 = jnp.exp(m1 - m), jnp.exp(m4 - m), jnp.exp(m16 - m)
            num = w1 * acc1 + w4 * acc4_s[rows, :] + w16 * acc16_s[rows, :]
            den = w1 * den1 + w4 * den4_s[rows, :] + w16 * den16_s[rows, :]
            o_ref[rows, :] = (num / den).astype(o_ref.dtype)


def _attn_prompt(q, k, v, batch, seq):
    for window, dil in DILATED_PATTERNS:
        assert window // dil == N_BACK and seq % (N_BACK * dil) == 0
    pair = pl.BlockSpec((None, seq, LANES), lambda b, p: (b, 0, p))
    return pl.pallas_call(
        _attn_prompt_kernel,
        grid=(batch, N_PAIRS),
        in_specs=[pair, pair, pair],
        out_specs=pair,
        out_shape=jax.ShapeDtypeStruct((batch, seq, ATT_WIDTH), BF16),
        scratch_shapes=[pltpu.VMEM((seq, LANES), F32)] * 6,
        compiler_params=pltpu.CompilerParams(dimension_semantics=("arbitrary", "arbitrary"),
                                             vmem_limit_bytes=VMEM_LIMIT),
        name="attn_prompt",
    )(q, k, v)


def _attn_sample_kernel(q_ref, kn_ref, vn_ref, kt_ref, vt_ref, o_ref):
    T = q_ref.shape[0]
    lb = kt_ref.shape[1]
    (_, _), (w4, d4), (_, d16) = DILATED_PATTERNS
    na = w4
    q = q_ref[...]
    kn, vn = kn_ref[...], vn_ref[...]
    lane8 = lax.broadcasted_iota(jnp.int32, (N_ATT_HEADS, ATT_WIDTH), 1)
    head8 = lax.broadcasted_iota(jnp.int32, (N_ATT_HEADS, ATT_WIDTH), 0)
    own = (lane8 // HEAD_DIM) == head8
    qrows = jnp.concatenate(
        [jnp.where(own, jnp.broadcast_to(q[t:t + 1, :], own.shape), 0.0) for t in range(T)], axis=0)
    nr = qrows.shape[0]
    qb = qrows.astype(BF16)
    trow = lax.broadcasted_iota(jnp.int32, (nr, 1), 0) // N_ATT_HEADS
    s_all = jnp.dot(qb, kt_ref[...].astype(BF16), preferred_element_type=F32)
    s_a = s_all[:, lb - na:]
    vt = vt_ref[...].astype(BF16)
    knb = kn.astype(BF16).astype(F32)
    s_new = [jnp.sum(qb.astype(F32) * knb[t:t + 1, :], axis=-1, keepdims=True) for t in range(T)]
    col = lax.broadcasted_iota(jnp.int32, (nr, na), 1)
    trow_b = jnp.broadcast_to(trow, (nr, na))
    vnb = vn.astype(BF16).astype(F32)

    def partial_softmax(s_cache, new_ok):
        m = jnp.max(s_cache, axis=-1, keepdims=True)
        for t in range(T):
            m = jnp.maximum(m, jnp.where(new_ok(t), s_new[t], -jnp.inf))
        p = jnp.exp(s_cache - m)
        den = jnp.sum(p, axis=-1, keepdims=True)
        p_new = [jnp.where(new_ok(t), jnp.exp(s_new[t] - m), 0.0) for t in range(T)]
        return p.astype(BF16), p_new, m, den + sum(p_new)

    pos = lax.broadcasted_iota(jnp.int32, (nr, lb), 1)
    patterns = [
        partial_softmax(jnp.where(col >= na - N_BACK + trow_b, s_a, -jnp.inf), lambda t: trow >= t),
        partial_softmax(jnp.where((col % d4) == trow_b, s_a, -jnp.inf), lambda t: trow == t),
        partial_softmax(jnp.where((pos % d16) == jnp.broadcast_to(trow, (nr, lb)), s_all, -jnp.inf),
                        lambda t: trow == t),
    ]
    widen = lambda p: p if p.shape[1] == lb else jnp.concatenate([jnp.zeros((nr, lb - na), BF16), p], axis=1)
    p_all = jnp.concatenate([widen(p) for p, _, _, _ in patterns], axis=0)
    acc_all = lax.dot_general(p_all, vt, _NT, preferred_element_type=F32)
    m = jnp.maximum(jnp.maximum(patterns[0][2], patterns[1][2]), patterns[2][2])
    num, den = 0.0, 0.0
    for i, (_, p_new, m_i, den_i) in enumerate(patterns):
        acc = acc_all[i * nr:(i + 1) * nr]
        for t in range(T):
            acc = acc + p_new[t].astype(BF16).astype(F32) * vnb[t:t + 1, :]
        w = jnp.exp(m_i - m)
        num, den = num + w * acc, den + w * den_i
    merged = num / den
    for t in range(T):
        rows = slice(t * N_ATT_HEADS, (t + 1) * N_ATT_HEADS)
        o_ref[t:t + 1, :] = jnp.sum(jnp.where(own, merged[rows], 0.0), axis=0, keepdims=True).astype(o_ref.dtype)


def _attn_sample(q, k_new, v_new, cache_kt, cache_vt):
    bd, T, _ = q.shape
    lb = cache_kt.shape[2]
    (_, _), (w4, d4), (w16, d16) = DILATED_PATTERNS
    assert lb == w16 and lb % d16 == 0 and w4 % d4 == 0 and T <= d4
    new_spec = pl.BlockSpec((None, T, ATT_WIDTH), lambda b: (b, 0, 0))
    cache_spec = pl.BlockSpec((None, ATT_WIDTH, lb), lambda b: (b, 0, 0))
    return pl.pallas_call(
        _attn_sample_kernel,
        grid=(bd,),
        in_specs=[new_spec, new_spec, new_spec, cache_spec, cache_spec],
        out_specs=new_spec,
        out_shape=jax.ShapeDtypeStruct((bd, T, ATT_WIDTH), BF16),
        compiler_params=pltpu.CompilerParams(dimension_semantics=("arbitrary",), vmem_limit_bytes=VMEM_LIMIT),
        name="attn_sample",
    )(q, k_new, v_new, cache_kt, cache_vt)


def _mlstm_kernel(q_ref, k_ref, v_ref, mo_ref, g_ref, gain_ref, c0_ref, n0_ref, m0_ref,
                  hm_ref, c_ref, n_ref, m_ref, *, chunk):
    L = chunk
    heads = range(N_ML_HEADS)
    c_ref[...] = c0_ref[...]
    n_ref[...] = n0_ref[...]
    m_ref[...] = m0_ref[...]
    row = lax.broadcasted_iota(jnp.int32, (L, L), 0)
    col = lax.broadcasted_iota(jnp.int32, (L, L), 1)
    tril = col <= row
    tril_f = tril.astype(F32)
    eye8 = (lax.broadcasted_iota(jnp.int32, (8, LANES), 0) == lax.broadcasted_iota(jnp.int32, (8, LANES), 1)).astype(F32)
    hi = lax.Precision.HIGHEST

    def body(c, carry):
        rows = pl.ds(pl.multiple_of(c * L, L), L)
        g = g_ref[rows, :]
        b_all = jnp.dot(tril_f, g, preferred_element_type=F32, precision=hi)
        g_rows = lax.dot_general(eye8, g, _NT, preferred_element_type=F32, precision=hi)
        b_rows = lax.dot_general(eye8, b_all, _NT, preferred_element_type=F32, precision=hi)
        q = [q_ref[h, rows, :] for h in heads]
        k = [k_ref[h, rows, :] for h in heads]
        v = [v_ref[h, rows, :] for h in heads]
        bcol = [b_all[:, N_ML_HEADS + h:N_ML_HEADS + h + 1] for h in heads]
        igcol = [g[:, h:h + 1] for h in heads]
        d = [jnp.where(tril, bcol[h] - b_rows[N_ML_HEADS + h:N_ML_HEADS + h + 1, :] + g_rows[h:h + 1, :], -jnp.inf)
             for h in heads]
        m_in = [jnp.max(d[h], axis=-1, keepdims=True) for h in heads]
        qk = [lax.dot_general(q[h], k[h], _NT, preferred_element_type=F32) for h in heads]
        a = [jnp.exp(d[h] - m_in[h]) * qk[h] for h in heads]
        num_in = [jnp.dot(a[h].astype(BF16), v[h], preferred_element_type=F32) for h in heads]
        den_in = [jnp.sum(a[h], axis=-1, keepdims=True) for h in heads]
        b_last = [bcol[h][L - 1:L, :] for h in heads]
        gcol = [b_last[h] - bcol[h] + igcol[h] for h in heads]
        g_max = [jnp.max(gcol[h], axis=0, keepdims=True) for h in heads]
        w_s = [jnp.exp(gcol[h] - g_max[h]) for h in heads]
        kv = [lax.dot_general(k[h], (w_s[h] * v[h].astype(F32)).astype(BF16), _TN, preferred_element_type=F32)
              for h in heads]
        k_sum = [jnp.sum(w_s[h] * k[h].astype(F32), axis=0, keepdims=True) for h in heads]
        m_prev = [m_ref[h][:, 0:1] for h in heads]
        cmat = [c_ref[h] for h in heads]
        nvec = [n_ref[h] for h in heads]
        q_c = [jnp.dot(q[h], cmat[h].astype(BF16), preferred_element_type=F32) for h in heads]
        q_n = [jnp.sum(q[h].astype(F32) * nvec[h], axis=-1, keepdims=True) for h in heads]
        for h in heads:
            inter = bcol[h] + m_prev[h]
            m_t = jnp.maximum(m_in[h], inter)
            f_in, f_st = jnp.exp(m_in[h] - m_t), jnp.exp(inter - m_t)
            num = f_in * num_in[h] + f_st * q_c[h]
            den = f_in * den_in[h] + f_st * q_n[h]
            den = jnp.maximum(jnp.abs(den), jnp.exp(-m_t))
            y = _rms(num / den, gain_ref[h]) * jax.nn.sigmoid(mo_ref[h, rows, :])
            hm_ref[rows, h * ML_DK:(h + 1) * ML_DK] = y.astype(hm_ref.dtype)
            m_new = jnp.maximum(b_last[h] + m_prev[h], g_max[h])
            w_c = jnp.exp(b_last[h] + m_prev[h] - m_new)
            w_k = jnp.exp(g_max[h] - m_new)
            c_ref[h] = w_c * cmat[h] + w_k * kv[h]
            n_ref[h] = w_c * nvec[h] + w_k * k_sum[h]
            m_ref[h] = jnp.broadcast_to(m_new, (1, LANES))
        return carry

    lax.fori_loop(0, q_ref.shape[1] // L, body, 0)


def _mlstm(mq, mk, mv, mo, gates, gain, c0, n0, m0, chunk):
    nh, batch, seq, _ = mq.shape
    per_head = pl.BlockSpec((nh, None, seq, LANES), lambda b: (0, b, 0, 0))
    state = lambda r: pl.BlockSpec((None, nh, r, LANES), lambda b: (b, 0, 0, 0))
    return pl.pallas_call(
        functools.partial(_mlstm_kernel, chunk=chunk),
        grid=(batch,),
        in_specs=[per_head, per_head, per_head, per_head,
                  pl.BlockSpec((None, seq, LANES), lambda b: (b, 0, 0)),
                  pl.BlockSpec((nh, 1, LANES), lambda b: (0, 0, 0)),
                  state(ML_DK), state(1), state(1)],
        out_specs=[pl.BlockSpec((None, seq, ML_WIDTH), lambda b: (b, 0, 0)), state(ML_DK), state(1), state(1)],
        out_shape=[jax.ShapeDtypeStruct((batch, seq, ML_WIDTH), BF16),
                   jax.ShapeDtypeStruct((batch, nh, ML_DK, LANES), F32),
                   jax.ShapeDtypeStruct((batch, nh, 1, LANES), F32),
                   jax.ShapeDtypeStruct((batch, nh, 1, LANES), F32)],
        compiler_params=pltpu.CompilerParams(dimension_semantics=("arbitrary",), vmem_limit_bytes=VMEM_LIMIT),
        name="mlstm",
    )(mq, mk, mv, mo, gates, gain, c0, n0, m0)


def _rope_tables(pos):
    half = HEAD_DIM // 2
    inv = ROPE_THETA ** (-jnp.arange(half, dtype=F32) / half)
    ang = pos.astype(F32)[:, None] * inv[None, :]
    cos, sin = jnp.cos(ang), jnp.sin(ang)
    return jnp.tile(jnp.concatenate([cos, cos], axis=-1), (1, 2)), jnp.tile(jnp.concatenate([-sin, sin], axis=-1), (1, 2))


def _layer_weights(l, ffn1_norm, ffn1_w_gate, ffn1_w_up, ffn1_w_down, mix_norm, w_in, q_norm, k_norm, b_igate,
                   b_fgate, ml_out_norm, w_out, ffn2_norm, ffn2_w_gate, ffn2_w_up, ffn2_w_down):
    pad_lanes = lambda a: jnp.pad(a, ((0, 0), (0, LANES - a.shape[1])))
    return {
        "g1": ffn1_norm[l][None, :], "wg1": ffn1_w_gate[l].astype(BF16), "wu1": ffn1_w_up[l].astype(BF16),
        "wd1": ffn1_w_down[l].astype(BF16), "gm": mix_norm[l][None, :],
        "win": w_in[l][:, :MAIN_WIDTH].astype(BF16), "wgate": pad_lanes(w_in[l][:, MAIN_WIDTH:]).astype(BF16),
        "bgate": pad_lanes(jnp.concatenate([b_igate[l], b_fgate[l]])[None, :]),
        "qg": jnp.tile(q_norm[l], 2)[None, :], "kg": jnp.tile(k_norm[l], 2)[None, :],
        "gain": ml_out_norm[l][:, None, :], "wo": w_out[l].astype(BF16),
        "g2": ffn2_norm[l][None, :], "wg2": ffn2_w_gate[l].astype(BF16), "wu2": ffn2_w_up[l].astype(BF16),
        "wd2": ffn2_w_down[l].astype(BF16),
    }


def _prompt_layer(x, w):
    batch, seq, _ = x.shape
    n = batch * seq
    cos, sin = _rope_tables(jnp.arange(seq, dtype=jnp.int32))
    x1, q, k, v, kt, vt, mq, mk, mv, mo, gates = _front(x.reshape(n, D_MODEL), cos, sin, w, seq)
    seqs = lambda t: t.reshape(batch, seq, ATT_WIDTH)
    att = _attn_prompt(seqs(q), seqs(k), seqs(v), batch, seq)
    heads = lambda t: t.reshape(N_ML_HEADS, batch, seq, LANES)
    zeros = lambda r: jnp.zeros((batch, N_ML_HEADS, r, LANES), F32)
    hm, c, nv, m = _mlstm(heads(mq), heads(mk), heads(mv), heads(mo), gates.reshape(batch, seq, LANES), w["gain"],
                          zeros(ML_DK), zeros(1), zeros(1), min(ML_CHUNK, seq))
    y = _back(x1, att.reshape(n, ATT_WIDTH), hm.reshape(n, ML_WIDTH), w)
    kv = lambda t: t.reshape(batch, N_ATT_HEADS, HEAD_DIM, seq).transpose(0, 3, 1, 2)
    return y.reshape(batch, seq, D_MODEL), kv(kt), kv(vt), c, nv[:, :, 0, :], m[:, :, 0, 0]


def _sample_layer(x, w, cache_k, cache_v, c0, n0, m0):
    bd, T, _ = x.shape
    n = bd * T
    cos, sin = _rope_tables(PAST_LEN + jnp.arange(T, dtype=jnp.int32))
    cos, sin = jnp.tile(cos, (bd, 1)), jnp.tile(sin, (bd, 1))
    x1, q, k, v, _, _, mq, mk, mv, mo, gates = _front(x.reshape(n, D_MODEL), cos, sin, w, n)
    seqs = lambda t: t.reshape(bd, T, ATT_WIDTH)
    transposed = lambda c: c.transpose(0, 2, 3, 1).reshape(bd, ATT_WIDTH, c.shape[1])
    att = _attn_sample(seqs(q), seqs(k), seqs(v), transposed(cache_k), transposed(cache_v))
    tp = 8
    heads = lambda t: jnp.pad(t.reshape(N_ML_HEADS, bd, T, LANES), ((0, 0), (0, 0), (0, tp - T), (0, 0)))
    pad_gate = jnp.where(jnp.arange(LANES) < N_ML_HEADS, NEG_BIG, 0.0).astype(F32)
    gates = jnp.concatenate([gates.reshape(bd, T, LANES), jnp.broadcast_to(pad_gate, (bd, tp - T, LANES))], axis=1)
    hm, c, nv, m = _mlstm(heads(mq), heads(mk), heads(mv), heads(mo), gates, w["gain"], c0, n0[:, :, None, :],
                          jnp.broadcast_to(m0[:, :, None, None], (bd, N_ML_HEADS, 1, LANES)), tp)
    y = _back(x1, att.reshape(n, ATT_WIDTH), hm[:, :T].reshape(n, ML_WIDTH), w)
    kv = lambda t: t.reshape(bd, T, N_ATT_HEADS, HEAD_DIM)
    return y.reshape(bd, T, D_MODEL), kv(k), kv(v), c, nv[:, :, 0, :], m[:, :, 0, 0]


def kernel(x_prompt, x_sample, cache_k_win, cache_v_win, state_C, state_n, state_m, ffn1_norm, ffn1_w_gate,
           ffn1_w_up, ffn1_w_down, mix_norm, w_in, q_norm, k_norm, b_igate, b_fgate, ml_out_norm, w_out,
           ffn2_norm, ffn2_w_gate, ffn2_w_up, ffn2_w_down):
    depth = w_in.shape[0]
    yp, ys = x_prompt, x_sample
    outs = [[] for _ in range(10)]
    for l in range(depth):
        w = _layer_weights(l, ffn1_norm, ffn1_w_gate, ffn1_w_up, ffn1_w_down, mix_norm, w_in, q_norm, k_norm,
                           b_igate, b_fgate, ml_out_norm, w_out, ffn2_norm, ffn2_w_gate, ffn2_w_up, ffn2_w_down)
        yp, kp, vp, cp, npr, mp = _prompt_layer(yp, w)
        ys, ks, vs, cs, ns, ms = _sample_layer(ys, w, cache_k_win[l], cache_v_win[l],
                                               state_C[l], state_n[l], state_m[l])
        for acc, val in zip(outs, (kp, vp, ks, vs, cp, npr, mp, cs, ns, ms)):
            acc.append(val)
    return (yp, ys) + tuple(jnp.stack(o) for o in outs)
```

```python
import functools

import jax
import jax.numpy as jnp
from jax import lax
from jax.experimental import pallas as pl
from jax.experimental.pallas import tpu as pltpu

F32 = jnp.float32
BF16 = jnp.bfloat16

D_MODEL = 1024
HEAD_DIM = 64
N_ATT_HEADS = 8
ATT_WIDTH = N_ATT_HEADS * HEAD_DIM
N_ML_HEADS = 4
ML_DK = 128
ML_WIDTH = N_ML_HEADS * ML_DK
D_FF = 2816
ML_CHUNK = 128
NORM_EPS = 1e-6
ROPE_THETA = 10000.0
PAST_LEN = 8192
DILATED_PATTERNS = ((128, 1), (512, 4), (2048, 16))
N_BACK = 128
ATT_SCALE = HEAD_DIM ** -0.5
LOG2_E = 1.4426950408889634
Q_SCALE = ATT_SCALE * LOG2_E
MAIN_WIDTH = 3 * ATT_WIDTH + 4 * ML_WIDTH

LANES = 128
N_PAIRS = ATT_WIDTH // LANES
ROW_TILE = 256
FF_CHUNK = 256
ATT_GROUP = 8
VMEM_LIMIT = 56 * 1024 * 1024
NEG_BIG = -1e30

_NT = (((1,), (1,)), ((), ()))
_TN = (((0,), (0,)), ((), ()))


def _rms(x, g):
    return x * lax.rsqrt(jnp.mean(x * x, axis=-1, keepdims=True) + NORM_EPS) * g


def _swiglu(hn, wg_ref, wu_ref, wd_ref, act_ref):
    for c in range(D_FF // FF_CHUNK):
        sl = slice(c * FF_CHUNK, (c + 1) * FF_CHUNK)
        g = jnp.dot(hn, wg_ref[:, sl], preferred_element_type=F32)
        u = jnp.dot(hn, wu_ref[:, sl], preferred_element_type=F32)
        act_ref[:, sl] = (g * jax.nn.sigmoid(g) * u).astype(BF16)
    return jnp.dot(act_ref[...], wd_ref[...], preferred_element_type=F32)


def _pair_norm_rope(x, gain, cos, sin, lo):
    sq = x * x
    s_lo = jnp.sum(jnp.where(lo, sq, 0.0), axis=-1, keepdims=True)
    s_hi = jnp.sum(jnp.where(lo, 0.0, sq), axis=-1, keepdims=True)
    ms = jnp.where(lo, s_lo, s_hi) * (1.0 / HEAD_DIM)
    y = x * lax.rsqrt(ms + NORM_EPS) * gain
    lane = lax.broadcasted_iota(jnp.int32, y.shape, 1)
    first_half = (lane % HEAD_DIM) < (HEAD_DIM // 2)
    partner = jnp.where(first_half, pltpu.roll(y, LANES - HEAD_DIM // 2, 1), pltpu.roll(y, HEAD_DIM // 2, 1))
    return y * cos + partner * sin


def _front_kernel(x_ref, cos_ref, sin_ref, g1_ref, wg_ref, wu_ref, wd_ref, gm_ref, win_ref, wgate_ref,
                  bgate_ref, qg_ref, kg_ref,
                  x1_ref, q_ref, k_ref, v_ref, kt_ref, vt_ref, mq_ref, mk_ref, mv_ref, mo_ref, gates_ref,
                  act_ref):
    x = x_ref[...]
    hn = _rms(x, g1_ref[...]).astype(BF16)
    x1 = x + 0.5 * _swiglu(hn, wg_ref, wu_ref, wd_ref, act_ref)
    x1_ref[...] = x1
    h = _rms(x1, gm_ref[...]).astype(BF16)

    def proj(i):
        return jnp.dot(h, win_ref[:, i * ATT_WIDTH:(i + 1) * ATT_WIDTH], preferred_element_type=F32)

    lane = lax.broadcasted_iota(jnp.int32, (x.shape[0], LANES), 1)
    lo = lane < HEAD_DIM
    cos, sin = cos_ref[...], sin_ref[...]
    aq, ak, av = proj(0), proj(1), proj(2)
    pairs = [slice(p * LANES, (p + 1) * LANES) for p in range(N_PAIRS)]
    for sl in pairs:
        q_ref[:, sl] = _pair_norm_rope(aq[:, sl], qg_ref[...], cos, sin, lo) * Q_SCALE
    ak = jnp.concatenate([_pair_norm_rope(ak[:, sl], kg_ref[...], cos, sin, lo) for sl in pairs], axis=1)
    k_ref[...] = ak
    v_ref[...] = av
    kt_ref[...] = ak.T
    vt_ref[...] = av.T
    mq, mk, mv, mo = proj(3), proj(4), proj(5), proj(6)
    for p in range(N_ML_HEADS):
        sl = slice(p * ML_DK, (p + 1) * ML_DK)
        mq_ref[p] = mq[:, sl].astype(BF16)
        mk_ref[p] = (mk[:, sl] * (ML_DK ** -0.5)).astype(BF16)
        mv_ref[p] = mv[:, sl].astype(BF16)
        mo_ref[p] = mo[:, sl]
    zg = jnp.dot(h, wgate_ref[...], preferred_element_type=F32) + bgate_ref[...]
    log_sig = jnp.minimum(zg, 0.0) - jnp.log(1.0 + jnp.exp(-jnp.abs(zg)))
    gates_ref[...] = jnp.where(lane < N_ML_HEADS, zg, log_sig)


def _back_kernel(x1_ref, att_ref, hm_ref, wo_ref, g2_ref, wg_ref, wu_ref, wd_ref, y_ref, act_ref):
    x2 = (x1_ref[...]
          + jnp.dot(att_ref[...], wo_ref[:ATT_WIDTH, :], preferred_element_type=F32)
          + jnp.dot(hm_ref[...], wo_ref[ATT_WIDTH:, :], preferred_element_type=F32))
    hn = _rms(x2, g2_ref[...]).astype(BF16)
    y_ref[...] = x2 + 0.5 * _swiglu(hn, wg_ref, wu_ref, wd_ref, act_ref)


def _const_spec(shape):
    return pl.BlockSpec(shape, lambda i: (0,) * len(shape), pipeline_mode=pl.Buffered(1))


def _dense_params():
    return pltpu.CompilerParams(dimension_semantics=("arbitrary",), vmem_limit_bytes=VMEM_LIMIT)


def _front(x, cos, sin, w, seq):
    n = x.shape[0]
    tm = ROW_TILE
    blocks = seq // tm
    row = lambda width: pl.BlockSpec((tm, width), lambda i: (i, 0))
    rope = pl.BlockSpec((tm, LANES), lambda i: (i % blocks, 0))
    grouped = lambda g: pl.BlockSpec((g, tm, LANES), lambda i: (0, i, 0))
    gshape = lambda g, dt: jax.ShapeDtypeStruct((g, n, LANES), dt)
    transposed = pl.BlockSpec((None, ATT_WIDTH, tm), lambda i: (i // blocks, 0, i % blocks))
    tshape = jax.ShapeDtypeStruct((n // seq, ATT_WIDTH, seq), F32)
    return pl.pallas_call(
        _front_kernel,
        grid=(n // tm,),
        in_specs=[row(D_MODEL), rope, rope,
                  _const_spec((1, D_MODEL)), _const_spec((D_MODEL, D_FF)), _const_spec((D_MODEL, D_FF)),
                  _const_spec((D_FF, D_MODEL)), _const_spec((1, D_MODEL)), _const_spec((D_MODEL, MAIN_WIDTH)),
                  _const_spec((D_MODEL, LANES)), _const_spec((1, LANES)), _const_spec((1, LANES)),
                  _const_spec((1, LANES))],
        out_specs=[row(D_MODEL), row(ATT_WIDTH), row(ATT_WIDTH), row(ATT_WIDTH), transposed, transposed,
                   grouped(N_ML_HEADS), grouped(N_ML_HEADS), grouped(N_ML_HEADS), grouped(N_ML_HEADS), row(LANES)],
        out_shape=[jax.ShapeDtypeStruct((n, D_MODEL), F32), jax.ShapeDtypeStruct((n, ATT_WIDTH), F32),
                   jax.ShapeDtypeStruct((n, ATT_WIDTH), F32), jax.ShapeDtypeStruct((n, ATT_WIDTH), F32),
                   tshape, tshape, gshape(N_ML_HEADS, BF16), gshape(N_ML_HEADS, BF16), gshape(N_ML_HEADS, BF16),
                   gshape(N_ML_HEADS, F32), jax.ShapeDtypeStruct((n, LANES), F32)],
        scratch_shapes=[pltpu.VMEM((tm, D_FF), BF16)],
        compiler_params=_dense_params(),
        name="front",
    )(x, cos, sin, w["g1"], w["wg1"], w["wu1"], w["wd1"], w["gm"], w["win"], w["wgate"], w["bgate"],
      w["qg"], w["kg"])


def _back(x1, att, hm, w):
    n = x1.shape[0]
    tm = ROW_TILE
    row = lambda width: pl.BlockSpec((tm, width), lambda i: (i, 0))
    return pl.pallas_call(
        _back_kernel,
        grid=(n // tm,),
        in_specs=[row(D_MODEL), row(ATT_WIDTH), row(ML_WIDTH), _const_spec((D_MODEL, D_MODEL)),
                  _const_spec((1, D_MODEL)), _const_spec((D_MODEL, D_FF)), _const_spec((D_MODEL, D_FF)),
                  _const_spec((D_FF, D_MODEL))],
        out_specs=row(D_MODEL),
        out_shape=jax.ShapeDtypeStruct((n, D_MODEL), F32),
        scratch_shapes=[pltpu.VMEM((tm, D_FF), BF16)],
        compiler_params=_dense_params(),
        name="back",
    )(x1, att, hm, w["wo"], w["g2"], w["wg2"], w["wu2"], w["wd2"])


def _attend_group(jobs, lo):
    zero = jnp.zeros((), BF16)
    one = jnp.ones((), BF16)
    scores, values = [], []
    for q, k, v, _ in jobs:
        lo_k = lax.broadcasted_iota(jnp.int32, k.shape, 1) < HEAD_DIM
        k2 = jnp.concatenate([jnp.where(lo_k, k, zero), jnp.where(lo_k, zero, k)], axis=0)
        scores.append(lax.dot_general(q, k2, _NT, preferred_element_type=F32))
        ones_lo = lo_k.astype(F32).astype(BF16)
        ones_hi = one - ones_lo
        values.append(jnp.concatenate(
            [jnp.concatenate([jnp.where(lo_k, v, zero), ones_lo], axis=1),
             jnp.concatenate([jnp.where(lo_k, zero, v), ones_hi], axis=1)], axis=0))
    masked = []
    for (_, k, _, mask), s in zip(jobs, scores):
        nk = k.shape[0]
        masked.append([jnp.where(mask, s[:, h * nk:(h + 1) * nk].astype(BF16), jnp.asarray(-jnp.inf, BF16))
                       for h in range(2)])
    maxes = [[jnp.max(s, axis=-1, keepdims=True) for s in pair] for pair in masked]
    probs = [jnp.concatenate([jnp.exp2(s - m) for s, m in zip(pair, mx)], axis=1)
             for pair, mx in zip(masked, maxes)]
    sums = [jnp.dot(p, v2, preferred_element_type=F32) for p, v2 in zip(probs, values)]
    return [(r[:, :LANES], jnp.where(lo, mx[0].astype(F32), mx[1].astype(F32)), r[:, LANES:])
            for r, mx in zip(sums, maxes)]


def _attn_prompt_kernel(q_ref, k_ref, v_ref, o_ref, acc4_s, m4_s, den4_s, acc16_s, m16_s, den16_s):
    nb = N_BACK
    seq = q_ref.shape[0]
    lo = lax.broadcasted_iota(jnp.int32, (nb, LANES), 1) < HEAD_DIM
    r2 = lax.broadcasted_iota(jnp.int32, (nb, 2 * nb), 0)
    c2 = lax.broadcasted_iota(jnp.int32, (nb, 2 * nb), 1)
    band = (c2 >= r2) & (c2 <= r2 + nb)
    causal = (lax.broadcasted_iota(jnp.int32, (nb, nb), 1)
              <= lax.broadcasted_iota(jnp.int32, (nb, nb), 0))

    def group_jobs(dil, group):
        loaded = {}

        def load(ref, r, blk):
            key = (id(ref), r, blk)
            if key not in loaded:
                rows = pl.ds(r + blk * nb * dil, nb, stride=dil) if dil > 1 else pl.ds(blk * nb, nb)
                loaded[key] = ref[rows, :].astype(BF16)
            return loaded[key]

        jobs = []
        for r, n in group:
            q = load(q_ref, r, n)
            if n == 0:
                jobs.append((q, load(k_ref, r, 0), load(v_ref, r, 0), causal))
            else:
                jobs.append((q, jnp.concatenate([load(k_ref, r, n - 1), load(k_ref, r, n)], axis=0),
                             jnp.concatenate([load(v_ref, r, n - 1), load(v_ref, r, n)], axis=0), band))
        return jobs

    def dilated(dil, acc_s, m_s, den_s):
        blocks = [(r, n) for r in range(dil) for n in range(seq // (nb * dil))]
        for g in range(0, len(blocks), ATT_GROUP):
            group = blocks[g:g + ATT_GROUP]
            for (r, n), (acc, m, den) in zip(group, _attend_group(group_jobs(dil, group), lo)):
                rows = pl.ds(r + n * nb * dil, nb, stride=dil)
                acc_s[rows, :] = acc
                m_s[rows, :] = m
                den_s[rows, :] = den

    dilated(DILATED_PATTERNS[2][1], acc16_s, m16_s, den16_s)
    dilated(DILATED_PATTERNS[1][1], acc4_s, m4_s, den4_s)

    for g in range(0, seq // nb, ATT_GROUP):
        group = list(range(g, g + ATT_GROUP))
        for n, (acc1, m1, den1) in zip(group, _attend_group(group_jobs(1, [(0, n) for n in group]), lo)):
            rows = pl.ds(n * nb, nb)
            m4, m16 = m4_s[rows, :], m16_s[rows, :]
            m = jnp.maximum(jnp.maximum(m1, m4), m16)
            w1, w4, w16 = jnp.exp2(m1 - m), jnp.exp2(m4 - m), jnp.exp2(m16 - m)
            num = w1 * acc1 + w4 * acc4_s[rows, :] + w16 * acc16_s[rows, :]
            den = w1 * den1 + w4 * den4_s[rows, :] + w16 * den16_s[rows, :]
            o_ref[rows, :] = (num / den).astype(o_ref.dtype)


def _attn_prompt(q, k, v, batch, seq):
    for window, dil in DILATED_PATTERNS:
        assert window // dil == N_BACK and seq % (N_BACK * dil) == 0
    pair = pl.BlockSpec((None, seq, LANES), lambda b, p: (b, 0, p))
    return pl.pallas_call(
        _attn_prompt_kernel,
        grid=(batch, N_PAIRS),
        in_specs=[pair, pair, pair],
        out_specs=pair,
        out_shape=jax.ShapeDtypeStruct((batch, seq, ATT_WIDTH), BF16),
        scratch_shapes=[pltpu.VMEM((seq, LANES), F32)] * 6,
        compiler_params=pltpu.CompilerParams(dimension_semantics=("arbitrary", "arbitrary"),
                                             vmem_limit_bytes=VMEM_LIMIT),
        name="attn_prompt",
    )(q, k, v)


def _attn_sample_kernel(q_ref, kn_ref, vn_ref, kt_ref, vt_ref, o_ref):
    T = q_ref.shape[0]
    lb = kt_ref.shape[1]
    (_, _), (w4, d4), (_, d16) = DILATED_PATTERNS
    na = w4
    q = q_ref[...]
    kn, vn = kn_ref[...], vn_ref[...]
    lane8 = lax.broadcasted_iota(jnp.int32, (N_ATT_HEADS, ATT_WIDTH), 1)
    head8 = lax.broadcasted_iota(jnp.int32, (N_ATT_HEADS, ATT_WIDTH), 0)
    own = (lane8 // HEAD_DIM) == head8
    qrows = jnp.concatenate(
        [jnp.where(own, jnp.broadcast_to(q[t:t + 1, :], own.shape), 0.0) for t in range(T)], axis=0)
    nr = qrows.shape[0]
    qb = qrows.astype(BF16)
    trow = lax.broadcasted_iota(jnp.int32, (nr, 1), 0) // N_ATT_HEADS
    s_all = jnp.dot(qb, kt_ref[...].astype(BF16), preferred_element_type=F32)
    s_a = s_all[:, lb - na:]
    vt = vt_ref[...].astype(BF16)
    knb = kn.astype(BF16).astype(F32)
    s_new = [jnp.sum(qb.astype(F32) * knb[t:t + 1, :], axis=-1, keepdims=True) for t in range(T)]
    col = lax.broadcasted_iota(jnp.int32, (nr, na), 1)
    trow_b = jnp.broadcast_to(trow, (nr, na))
    vnb = vn.astype(BF16).astype(F32)

    def partial_softmax(s_cache, new_ok):
        m = jnp.max(s_cache, axis=-1, keepdims=True)
        for t in range(T):
            m = jnp.maximum(m, jnp.where(new_ok(t), s_new[t], -jnp.inf))
        p = jnp.exp2(s_cache - m)
        den = jnp.sum(p, axis=-1, keepdims=True)
        p_new = [jnp.where(new_ok(t), jnp.exp2(s_new[t] - m), 0.0) for t in range(T)]
        return p.astype(BF16), p_new, m, den + sum(p_new)

    pos = lax.broadcasted_iota(jnp.int32, (nr, lb), 1)
    patterns = [
        partial_softmax(jnp.where(col >= na - N_BACK + trow_b, s_a, -jnp.inf), lambda t: trow >= t),
        partial_softmax(jnp.where((col % d4) == trow_b, s_a, -jnp.inf), lambda t: trow == t),
        partial_softmax(jnp.where((pos % d16) == jnp.broadcast_to(trow, (nr, lb)), s_all, -jnp.inf),
                        lambda t: trow == t),
    ]
    widen = lambda p: p if p.shape[1] == lb else jnp.concatenate([jnp.zeros((nr, lb - na), BF16), p], axis=1)
    p_all = jnp.concatenate([widen(p) for p, _, _, _ in patterns], axis=0)
    acc_all = lax.dot_general(p_all, vt, _NT, preferred_element_type=F32)
    m = jnp.maximum(jnp.maximum(patterns[0][2], patterns[1][2]), patterns[2][2])
    num, den = 0.0, 0.0
    for i, (_, p_new, m_i, den_i) in enumerate(patterns):
        acc = acc_all[i * nr:(i + 1) * nr]
        for t in range(T):
            acc = acc + p_new[t].astype(BF16).astype(F32) * vnb[t:t + 1, :]
        w = jnp.exp2(m_i - m)
        num, den = num + w * acc, den + w * den_i
    merged = num / den
    for t in range(T):
        rows = slice(t * N_ATT_HEADS, (t + 1) * N_ATT_HEADS)
        o_ref[t:t + 1, :] = jnp.sum(jnp.where(own, merged[rows], 0.0), axis=0, keepdims=True).astype(o_ref.dtype)


def _attn_sample(q, k_new, v_new, cache_kt, cache_vt):
    bd, T, _ = q.shape
    lb = cache_kt.shape[2]
    (_, _), (w4, d4), (w16, d16) = DILATED_PATTERNS
    assert lb == w16 and lb % d16 == 0 and w4 % d4 == 0 and T <= d4
    new_spec = pl.BlockSpec((None, T, ATT_WIDTH), lambda b: (b, 0, 0))
    cache_spec = pl.BlockSpec((None, ATT_WIDTH, lb), lambda b: (b, 0, 0))
    return pl.pallas_call(
        _attn_sample_kernel,
        grid=(bd,),
        in_specs=[new_spec, new_spec, new_spec, cache_spec, cache_spec],
        out_specs=new_spec,
        out_shape=jax.ShapeDtypeStruct((bd, T, ATT_WIDTH), BF16),
        compiler_params=pltpu.CompilerParams(dimension_semantics=("arbitrary",), vmem_limit_bytes=VMEM_LIMIT),
        name="attn_sample",
    )(q, k_new, v_new, cache_kt, cache_vt)


def _mlstm_kernel(q_ref, k_ref, v_ref, mo_ref, g_ref, gain_ref, c0_ref, n0_ref, m0_ref,
                  hm_ref, c_ref, n_ref, m_ref, nt_s, *, chunk):
    L = chunk
    heads = range(N_ML_HEADS)
    c_ref[...] = c0_ref[...]
    m_ref[...] = m0_ref[...]
    eye = (lax.broadcasted_iota(jnp.int32, (ML_DK, LANES), 0) == lax.broadcasted_iota(jnp.int32, (ML_DK, LANES), 1))
    for h in heads:
        n_col = jnp.sum(jnp.where(eye, jnp.broadcast_to(n0_ref[h], (ML_DK, LANES)), 0.0), axis=-1, keepdims=True)
        nt_s[h] = jnp.broadcast_to(n_col, (ML_DK, LANES))
    lane = lax.broadcasted_iota(jnp.int32, (1, LANES), 1)
    sel_ig = [(lane == h).astype(F32) for h in heads]
    sel_lf = [(lane == h + N_ML_HEADS).astype(F32) for h in heads]
    row = lax.broadcasted_iota(jnp.int32, (L, L), 0)
    col = lax.broadcasted_iota(jnp.int32, (L, L), 1)
    tril = col <= row
    tril_f = tril.astype(F32)
    eye8 = (lax.broadcasted_iota(jnp.int32, (8, LANES), 0) == lax.broadcasted_iota(jnp.int32, (8, LANES), 1)).astype(F32)
    hi = lax.Precision.HIGHEST

    def body(c, carry):
        rows = pl.ds(pl.multiple_of(c * L, L), L)
        g = g_ref[rows, :]
        b_all = jnp.dot(tril_f, g, preferred_element_type=F32, precision=hi)
        g_rows = lax.dot_general(eye8, g, _NT, preferred_element_type=F32, precision=hi)
        b_rows = lax.dot_general(eye8, b_all, _NT, preferred_element_type=F32, precision=hi)
        q = [q_ref[h, rows, :] for h in heads]
        k = [k_ref[h, rows, :] for h in heads]
        v = [v_ref[h, rows, :] for h in heads]
        total = jnp.sum(g, axis=0, keepdims=True)
        back = total - b_all
        bcol = [jnp.sum(b_all * sel_lf[h], axis=-1, keepdims=True) for h in heads]
        gcol = [jnp.sum(back * sel_lf[h] + g * sel_ig[h], axis=-1, keepdims=True) for h in heads]
        b_last = [jnp.sum(total * sel_lf[h], axis=-1, keepdims=True) for h in heads]
        d = [jnp.where(tril, bcol[h] - b_rows[N_ML_HEADS + h:N_ML_HEADS + h + 1, :] + g_rows[h:h + 1, :], -jnp.inf)
             for h in heads]
        m_in = [jnp.max(d[h], axis=-1, keepdims=True) for h in heads]
        qk = [lax.dot_general(q[h], k[h], _NT, preferred_element_type=F32) for h in heads]
        a = [jnp.exp(d[h] - m_in[h]) * qk[h] for h in heads]
        num_in = [jnp.dot(a[h].astype(BF16), v[h], preferred_element_type=F32) for h in heads]
        den_in = [jnp.sum(a[h], axis=-1, keepdims=True) for h in heads]
        g_tile = [jnp.broadcast_to(gcol[h], (L, LANES)) for h in heads]
        g_max = [jnp.max(g_tile[h], axis=0, keepdims=True) for h in heads]
        w_s = [jnp.exp(g_tile[h] - g_max[h]) for h in heads]
        kvn = [lax.dot_general(
            k[h], jnp.concatenate([(w_s[h] * v[h].astype(F32)).astype(BF16), w_s[h].astype(BF16)], axis=1),
            _TN, preferred_element_type=F32) for h in heads]
        m_prev = [m_ref[h] for h in heads]
        cmat = [c_ref[h] for h in heads]
        ntile = [nt_s[h] for h in heads]
        q_cn = [jnp.dot(q[h], jnp.concatenate([cmat[h].astype(BF16), ntile[h].astype(BF16)], axis=1),
                        preferred_element_type=F32) for h in heads]
        for h in heads:
            inter = bcol[h] + m_prev[h]
            m_t = jnp.maximum(m_in[h], inter)
            f_in, f_st = jnp.exp(m_in[h] - m_t), jnp.exp(inter - m_t)
            num = f_in * num_in[h] + f_st * q_cn[h][:, :ML_DK]
            den = f_in * den_in[h] + f_st * q_cn[h][:, ML_DK:]
            den = jnp.maximum(jnp.abs(den), jnp.exp(-m_t))
            y = _rms(num / den, gain_ref[h]) * jax.nn.sigmoid(mo_ref[h, rows, :])
            hm_ref[rows, h * ML_DK:(h + 1) * ML_DK] = y.astype(hm_ref.dtype)
            m_new = jnp.maximum(b_last[h] + m_prev[h], g_max[h])
            w_c = jnp.exp(b_last[h] + m_prev[h] - m_new)
            w_k = jnp.exp(g_max[h] - m_new)
            c_ref[h] = w_c * cmat[h] + w_k * kvn[h][:, :ML_DK]
            nt_s[h] = w_c * ntile[h] + w_k * kvn[h][:, ML_DK:]
            m_ref[h] = m_new
        return carry

    n_chunks = q_ref.shape[1] // L
    lax.fori_loop(0, n_chunks, body, 0, unroll=2 if n_chunks % 2 == 0 else 1)
    for h in heads:
        n_ref[h] = jnp.sum(jnp.where(eye, nt_s[h], 0.0), axis=0, keepdims=True)


def _mlstm(mq, mk, mv, mo, gates, gain, c0, n0, m0, chunk):
    nh, batch, seq, _ = mq.shape
    per_head = pl.BlockSpec((nh, None, seq, LANES), lambda b: (0, b, 0, 0))
    state = lambda r: pl.BlockSpec((None, nh, r, LANES), lambda b: (b, 0, 0, 0))
    return pl.pallas_call(
        functools.partial(_mlstm_kernel, chunk=chunk),
        grid=(batch,),
        in_specs=[per_head, per_head, per_head, per_head,
                  pl.BlockSpec((None, seq, LANES), lambda b: (b, 0, 0)),
                  pl.BlockSpec((nh, 1, LANES), lambda b: (0, 0, 0)),
                  state(ML_DK), state(1), state(1)],
        out_specs=[pl.BlockSpec((None, seq, ML_WIDTH), lambda b: (b, 0, 0)), state(ML_DK), state(1), state(1)],
        out_shape=[jax.ShapeDtypeStruct((batch, seq, ML_WIDTH), BF16),
                   jax.ShapeDtypeStruct((batch, nh, ML_DK, LANES), F32),
                   jax.ShapeDtypeStruct((batch, nh, 1, LANES), F32),
                   jax.ShapeDtypeStruct((batch, nh, 1, LANES), F32)],
        scratch_shapes=[pltpu.VMEM((nh, ML_DK, LANES), F32)],
        compiler_params=pltpu.CompilerParams(dimension_semantics=("arbitrary",), vmem_limit_bytes=VMEM_LIMIT),
        name="mlstm",
    )(mq, mk, mv, mo, gates, gain, c0, n0, m0)


def _rope_tables(pos):
    half = HEAD_DIM // 2
    inv = ROPE_THETA ** (-jnp.arange(half, dtype=F32) / half)
    ang = pos.astype(F32)[:, None] * inv[None, :]
    cos, sin = jnp.cos(ang), jnp.sin(ang)
    return jnp.tile(jnp.concatenate([cos, cos], axis=-1), (1, 2)), jnp.tile(jnp.concatenate([-sin, sin], axis=-1), (1, 2))


def _layer_weights(l, ffn1_norm, ffn1_w_gate, ffn1_w_up, ffn1_w_down, mix_norm, w_in, q_norm, k_norm, b_igate,
                   b_fgate, ml_out_norm, w_out, ffn2_norm, ffn2_w_gate, ffn2_w_up, ffn2_w_down):
    pad_lanes = lambda a: jnp.pad(a, ((0, 0), (0, LANES - a.shape[1])))
    return {
        "g1": ffn1_norm[l][None, :], "wg1": ffn1_w_gate[l].astype(BF16), "wu1": ffn1_w_up[l].astype(BF16),
        "wd1": ffn1_w_down[l].astype(BF16), "gm": mix_norm[l][None, :],
        "win": w_in[l][:, :MAIN_WIDTH].astype(BF16), "wgate": pad_lanes(w_in[l][:, MAIN_WIDTH:]).astype(BF16),
        "bgate": pad_lanes(jnp.concatenate([b_igate[l], b_fgate[l]])[None, :]),
        "qg": jnp.tile(q_norm[l], 2)[None, :], "kg": jnp.tile(k_norm[l], 2)[None, :],
        "gain": ml_out_norm[l][:, None, :], "wo": w_out[l].astype(BF16),
        "g2": ffn2_norm[l][None, :], "wg2": ffn2_w_gate[l].astype(BF16), "wu2": ffn2_w_up[l].astype(BF16),
        "wd2": ffn2_w_down[l].astype(BF16),
    }


def _prompt_layer(x, w):
    batch, seq, _ = x.shape
    n = batch * seq
    cos, sin = _rope_tables(jnp.arange(seq, dtype=jnp.int32))
    x1, q, k, v, kt, vt, mq, mk, mv, mo, gates = _front(x.reshape(n, D_MODEL), cos, sin, w, seq)
    seqs = lambda t: t.reshape(batch, seq, ATT_WIDTH)
    att = _attn_prompt(seqs(q), seqs(k), seqs(v), batch, seq)
    heads = lambda t: t.reshape(N_ML_HEADS, batch, seq, LANES)
    zeros = lambda r: jnp.zeros((batch, N_ML_HEADS, r, LANES), F32)
    hm, c, nv, m = _mlstm(heads(mq), heads(mk), heads(mv), heads(mo), gates.reshape(batch, seq, LANES), w["gain"],
                          zeros(ML_DK), zeros(1), zeros(1), min(ML_CHUNK, seq))
    y = _back(x1, att.reshape(n, ATT_WIDTH), hm.reshape(n, ML_WIDTH), w)
    kv = lambda t: t.reshape(batch, N_ATT_HEADS, HEAD_DIM, seq).transpose(0, 3, 1, 2)
    return y.reshape(batch, seq, D_MODEL), kv(kt), kv(vt), c, nv[:, :, 0, :], m[:, :, 0, 0]


def _sample_layer(x, w, cache_k, cache_v, c0, n0, m0):
    bd, T, _ = x.shape
    n = bd * T
    cos, sin = _rope_tables(PAST_LEN + jnp.arange(T, dtype=jnp.int32))
    cos, sin = jnp.tile(cos, (bd, 1)), jnp.tile(sin, (bd, 1))
    x1, q, k, v, _, _, mq, mk, mv, mo, gates = _front(x.reshape(n, D_MODEL), cos, sin, w, n)
    seqs = lambda t: t.reshape(bd, T, ATT_WIDTH)
    transposed = lambda c: c.transpose(0, 2, 3, 1).reshape(bd, ATT_WIDTH, c.shape[1])
    att = _attn_sample(seqs(q), seqs(k), seqs(v), transposed(cache_k), transposed(cache_v))
    tp = 8
    heads = lambda t: jnp.pad(t.reshape(N_ML_HEADS, bd, T, LANES), ((0, 0), (0, 0), (0, tp - T), (0, 0)))
    pad_gate = jnp.where(jnp.arange(LANES) < N_ML_HEADS, NEG_BIG, 0.0).astype(F32)
    gates = jnp.concatenate([gates.reshape(bd, T, LANES), jnp.broadcast_to(pad_gate, (bd, tp - T, LANES))], axis=1)
    hm, c, nv, m = _mlstm(heads(mq), heads(mk), heads(mv), heads(mo), gates, w["gain"], c0, n0[:, :, None, :],
                          jnp.broadcast_to(m0[:, :, None, None], (bd, N_ML_HEADS, 1, LANES)), tp)
    y = _back(x1, att.reshape(n, ATT_WIDTH), hm[:, :T].reshape(n, ML_WIDTH), w)
    kv = lambda t: t.reshape(bd, T, N_ATT_HEADS, HEAD_DIM)
    return y.reshape(bd, T, D_MODEL), kv(k), kv(v), c, nv[:, :, 0, :], m[:, :, 0, 0]


def kernel(x_prompt, x_sample, cache_k_win, cache_v_win, state_C, state_n, state_m, ffn1_norm, ffn1_w_gate,
           ffn1_w_up, ffn1_w_down, mix_norm, w_in, q_norm, k_norm, b_igate, b_fgate, ml_out_norm, w_out,
           ffn2_norm, ffn2_w_gate, ffn2_w_up, ffn2_w_down):
    depth = w_in.shape[0]
    yp, ys = x_prompt, x_sample
    outs = [[] for _ in range(10)]
    for l in range(depth):
        w = _layer_weights(l, ffn1_norm, ffn1_w_gate, ffn1_w_up, ffn1_w_down, mix_norm, w_in, q_norm, k_norm,
                           b_igate, b_fgate, ml_out_norm, w_out, ffn2_norm, ffn2_w_gate, ffn2_w_up, ffn2_w_down)
        yp, kp, vp, cp, npr, mp = _prompt_layer(yp, w)
        ys, ks, vs, cs, ns, ms = _sample_layer(ys, w, cache_k_win[l], cache_v_win[l],
                                               state_C[l], state_n[l], state_m[l])
        for acc, val in zip(outs, (kp, vp, ks, vs, cp, npr, mp, cs, ns, ms)):
            acc.append(val)
    return (yp, ys) + tuple(jnp.stack(o) for o in outs)
```

```python
import functools

import jax
import jax.numpy as jnp
from jax import lax
from jax.experimental import pallas as pl
from jax.experimental.pallas import tpu as pltpu

F32 = jnp.float32
BF16 = jnp.bfloat16

D_MODEL = 1024
HEAD_DIM = 64
N_ATT_HEADS = 8
ATT_WIDTH = N_ATT_HEADS * HEAD_DIM
N_ML_HEADS = 4
ML_DK = 128
ML_WIDTH = N_ML_HEADS * ML_DK
D_FF = 2816
ML_CHUNK = 128
NORM_EPS = 1e-6
ROPE_THETA = 10000.0
PAST_LEN = 8192
DILATED_PATTERNS = ((128, 1), (512, 4), (2048, 16))
N_BACK = 128
ATT_SCALE = HEAD_DIM ** -0.5
LOG2_E = 1.4426950408889634
Q_SCALE = ATT_SCALE * LOG2_E
MAIN_WIDTH = 3 * ATT_WIDTH + 4 * ML_WIDTH

LANES = 128
N_PAIRS = ATT_WIDTH // LANES
ROW_TILE = 256
FF_CHUNK = 256
ATT_GROUP = 8
VMEM_LIMIT = 56 * 1024 * 1024
NEG_BIG = -1e30

_NT = (((1,), (1,)), ((), ()))
_TN = (((0,), (0,)), ((), ()))


def _split3(x):
    hi = x.astype(BF16)
    rest = x - hi.astype(F32)
    mid = rest.astype(BF16)
    return hi, mid, (rest - mid.astype(F32)).astype(BF16)


def _rms(x, g):
    return x * lax.rsqrt(jnp.mean(x * x, axis=-1, keepdims=True) + NORM_EPS) * g


def _swiglu(hn, wg_ref, wu_ref, wd_ref, act_ref):
    for c in range(D_FF // FF_CHUNK):
        sl = slice(c * FF_CHUNK, (c + 1) * FF_CHUNK)
        g = jnp.dot(hn, wg_ref[:, sl], preferred_element_type=F32)
        u = jnp.dot(hn, wu_ref[:, sl], preferred_element_type=F32)
        act_ref[:, sl] = (g * jax.nn.sigmoid(g) * u).astype(BF16)
    return jnp.dot(act_ref[...], wd_ref[...], preferred_element_type=F32)


def _pair_norm_rope(x, gain, cos, sin, lo):
    sq = x * x
    s_lo = jnp.sum(jnp.where(lo, sq, 0.0), axis=-1, keepdims=True)
    s_hi = jnp.sum(jnp.where(lo, 0.0, sq), axis=-1, keepdims=True)
    ms = jnp.where(lo, s_lo, s_hi) * (1.0 / HEAD_DIM)
    y = x * lax.rsqrt(ms + NORM_EPS) * gain
    lane = lax.broadcasted_iota(jnp.int32, y.shape, 1)
    first_half = (lane % HEAD_DIM) < (HEAD_DIM // 2)
    partner = jnp.where(first_half, pltpu.roll(y, LANES - HEAD_DIM // 2, 1), pltpu.roll(y, HEAD_DIM // 2, 1))
    return y * cos + partner * sin


def _front_kernel(x_ref, cos_ref, sin_ref, g1_ref, wg_ref, wu_ref, wd_ref, gm_ref, win_ref, wgate_ref,
                  bgate_ref, qg_ref, kg_ref,
                  x1_ref, q_ref, k_ref, v_ref, kt_ref, vt_ref, mq_ref, mk_ref, mv_ref, mo_ref, gates_ref,
                  act_ref):
    x = x_ref[...]
    hn = _rms(x, g1_ref[...]).astype(BF16)
    x1 = x + 0.5 * _swiglu(hn, wg_ref, wu_ref, wd_ref, act_ref)
    x1_ref[...] = x1
    h = _rms(x1, gm_ref[...]).astype(BF16)

    def proj(i):
        return jnp.dot(h, win_ref[:, i * ATT_WIDTH:(i + 1) * ATT_WIDTH], preferred_element_type=F32)

    lane = lax.broadcasted_iota(jnp.int32, (x.shape[0], LANES), 1)
    lo = lane < HEAD_DIM
    cos, sin = cos_ref[...], sin_ref[...]
    aq, ak, av = proj(0), proj(1), proj(2)
    pairs = [slice(p * LANES, (p + 1) * LANES) for p in range(N_PAIRS)]
    for sl in pairs:
        q_ref[:, sl] = _pair_norm_rope(aq[:, sl], qg_ref[...], cos, sin, lo) * Q_SCALE
    ak = jnp.concatenate([_pair_norm_rope(ak[:, sl], kg_ref[...], cos, sin, lo) for sl in pairs], axis=1)
    k_ref[...] = ak
    v_ref[...] = av
    kt_ref[...] = ak.T
    vt_ref[...] = av.T
    mq, mk, mv, mo = proj(3), proj(4), proj(5), proj(6)
    for p in range(N_ML_HEADS):
        sl = slice(p * ML_DK, (p + 1) * ML_DK)
        mq_ref[p] = mq[:, sl].astype(BF16)
        mk_ref[p] = (mk[:, sl] * (ML_DK ** -0.5)).astype(BF16)
        mv_ref[p] = mv[:, sl].astype(BF16)
        mo_ref[p] = mo[:, sl]
    zg = jnp.dot(h, wgate_ref[...], preferred_element_type=F32) + bgate_ref[...]
    log_sig = jnp.minimum(zg, 0.0) - jnp.log(1.0 + jnp.exp(-jnp.abs(zg)))
    gates_ref[...] = jnp.where(lane < N_ML_HEADS, zg, log_sig)


def _back_kernel(x1_ref, att_ref, hm_ref, wo_ref, g2_ref, wg_ref, wu_ref, wd_ref, y_ref, act_ref):
    x2 = (x1_ref[...]
          + jnp.dot(att_ref[...], wo_ref[:ATT_WIDTH, :], preferred_element_type=F32)
          + jnp.dot(hm_ref[...], wo_ref[ATT_WIDTH:, :], preferred_element_type=F32))
    hn = _rms(x2, g2_ref[...]).astype(BF16)
    y_ref[...] = x2 + 0.5 * _swiglu(hn, wg_ref, wu_ref, wd_ref, act_ref)


def _const_spec(shape):
    return pl.BlockSpec(shape, lambda i: (0,) * len(shape), pipeline_mode=pl.Buffered(1))


def _dense_params():
    return pltpu.CompilerParams(dimension_semantics=("arbitrary",), vmem_limit_bytes=VMEM_LIMIT)


def _front(x, cos, sin, w, seq):
    n = x.shape[0]
    tm = ROW_TILE
    blocks = seq // tm
    row = lambda width: pl.BlockSpec((tm, width), lambda i: (i, 0))
    rope = pl.BlockSpec((tm, LANES), lambda i: (i % blocks, 0))
    grouped = lambda g: pl.BlockSpec((g, tm, LANES), lambda i: (0, i, 0))
    gshape = lambda g, dt: jax.ShapeDtypeStruct((g, n, LANES), dt)
    transposed = pl.BlockSpec((None, ATT_WIDTH, tm), lambda i: (i // blocks, 0, i % blocks))
    tshape = jax.ShapeDtypeStruct((n // seq, ATT_WIDTH, seq), F32)
    return pl.pallas_call(
        _front_kernel,
        grid=(n // tm,),
        in_specs=[row(D_MODEL), rope, rope,
                  _const_spec((1, D_MODEL)), _const_spec((D_MODEL, D_FF)), _const_spec((D_MODEL, D_FF)),
                  _const_spec((D_FF, D_MODEL)), _const_spec((1, D_MODEL)), _const_spec((D_MODEL, MAIN_WIDTH)),
                  _const_spec((D_MODEL, LANES)), _const_spec((1, LANES)), _const_spec((1, LANES)),
                  _const_spec((1, LANES))],
        out_specs=[row(D_MODEL), row(ATT_WIDTH), row(ATT_WIDTH), row(ATT_WIDTH), transposed, transposed,
                   grouped(N_ML_HEADS), grouped(N_ML_HEADS), grouped(N_ML_HEADS), grouped(N_ML_HEADS), row(LANES)],
        out_shape=[jax.ShapeDtypeStruct((n, D_MODEL), F32), jax.ShapeDtypeStruct((n, ATT_WIDTH), F32),
                   jax.ShapeDtypeStruct((n, ATT_WIDTH), F32), jax.ShapeDtypeStruct((n, ATT_WIDTH), F32),
                   tshape, tshape, gshape(N_ML_HEADS, BF16), gshape(N_ML_HEADS, BF16), gshape(N_ML_HEADS, BF16),
                   gshape(N_ML_HEADS, F32), jax.ShapeDtypeStruct((n, LANES), F32)],
        scratch_shapes=[pltpu.VMEM((tm, D_FF), BF16)],
        compiler_params=_dense_params(),
        name="front",
    )(x, cos, sin, w["g1"], w["wg1"], w["wu1"], w["wd1"], w["gm"], w["win"], w["wgate"], w["bgate"],
      w["qg"], w["kg"])


def _back(x1, att, hm, w):
    n = x1.shape[0]
    tm = ROW_TILE
    row = lambda width: pl.BlockSpec((tm, width), lambda i: (i, 0))
    return pl.pallas_call(
        _back_kernel,
        grid=(n // tm,),
        in_specs=[row(D_MODEL), row(ATT_WIDTH), row(ML_WIDTH), _const_spec((D_MODEL, D_MODEL)),
                  _const_spec((1, D_MODEL)), _const_spec((D_MODEL, D_FF)), _const_spec((D_MODEL, D_FF)),
                  _const_spec((D_FF, D_MODEL))],
        out_specs=row(D_MODEL),
        out_shape=jax.ShapeDtypeStruct((n, D_MODEL), F32),
        scratch_shapes=[pltpu.VMEM((tm, D_FF), BF16)],
        compiler_params=_dense_params(),
        name="back",
    )(x1, att, hm, w["wo"], w["g2"], w["wg2"], w["wu2"], w["wd2"])


def _attend_group(jobs, lo):
    zero = jnp.zeros((), BF16)
    one = jnp.ones((), BF16)
    scores, values = [], []
    for q, k, v, _ in jobs:
        lo_k = lax.broadcasted_iota(jnp.int32, k.shape, 1) < HEAD_DIM
        k2 = jnp.concatenate([jnp.where(lo_k, k, zero), jnp.where(lo_k, zero, k)], axis=0)
        scores.append(lax.dot_general(q, k2, _NT, preferred_element_type=F32))
        ones_lo = lo_k.astype(F32).astype(BF16)
        ones_hi = one - ones_lo
        values.append(jnp.concatenate(
            [jnp.concatenate([jnp.where(lo_k, v, zero), ones_lo], axis=1),
             jnp.concatenate([jnp.where(lo_k, zero, v), ones_hi], axis=1)], axis=0))
    masked = []
    for (_, k, _, mask), s in zip(jobs, scores):
        nk = k.shape[0]
        masked.append([jnp.where(mask, s[:, h * nk:(h + 1) * nk].astype(BF16), jnp.asarray(-jnp.inf, BF16))
                       for h in range(2)])
    maxes = [[jnp.max(s, axis=-1, keepdims=True) for s in pair] for pair in masked]
    probs = [jnp.concatenate([jnp.exp2(s - m) for s, m in zip(pair, mx)], axis=1)
             for pair, mx in zip(masked, maxes)]
    sums = [jnp.dot(p, v2, preferred_element_type=F32) for p, v2 in zip(probs, values)]
    return [(r[:, :LANES], jnp.where(lo, mx[0].astype(F32), mx[1].astype(F32)), r[:, LANES:])
            for r, mx in zip(sums, maxes)]


def _attn_prompt_kernel(q_ref, k_ref, v_ref, o_ref, acc4_s, m4_s, den4_s, acc16_s, m16_s, den16_s):
    nb = N_BACK
    seq = q_ref.shape[0]
    lo = lax.broadcasted_iota(jnp.int32, (nb, LANES), 1) < HEAD_DIM
    r2 = lax.broadcasted_iota(jnp.int32, (nb, 2 * nb), 0)
    c2 = lax.broadcasted_iota(jnp.int32, (nb, 2 * nb), 1)
    band = (c2 >= r2) & (c2 <= r2 + nb)
    causal = (lax.broadcasted_iota(jnp.int32, (nb, nb), 1)
              <= lax.broadcasted_iota(jnp.int32, (nb, nb), 0))

    def group_jobs(dil, group):
        loaded = {}

        def load(ref, r, blk):
            key = (id(ref), r, blk)
            if key not in loaded:
                rows = pl.ds(r + blk * nb * dil, nb, stride=dil) if dil > 1 else pl.ds(blk * nb, nb)
                loaded[key] = ref[rows, :].astype(BF16)
            return loaded[key]

        jobs = []
        for r, n in group:
            q = load(q_ref, r, n)
            if n == 0:
                jobs.append((q, load(k_ref, r, 0), load(v_ref, r, 0), causal))
            else:
                jobs.append((q, jnp.concatenate([load(k_ref, r, n - 1), load(k_ref, r, n)], axis=0),
                             jnp.concatenate([load(v_ref, r, n - 1), load(v_ref, r, n)], axis=0), band))
        return jobs

    def dilated(dil, acc_s, m_s, den_s):
        blocks = [(r, n) for r in range(dil) for n in range(seq // (nb * dil))]
        for g in range(0, len(blocks), ATT_GROUP):
            group = blocks[g:g + ATT_GROUP]
            for (r, n), (acc, m, den) in zip(group, _attend_group(group_jobs(dil, group), lo)):
                rows = pl.ds(r + n * nb * dil, nb, stride=dil)
                acc_s[rows, :] = acc
                m_s[rows, :] = m
                den_s[rows, :] = den

    dilated(DILATED_PATTERNS[2][1], acc16_s, m16_s, den16_s)
    dilated(DILATED_PATTERNS[1][1], acc4_s, m4_s, den4_s)

    for g in range(0, seq // nb, ATT_GROUP):
        group = list(range(g, g + ATT_GROUP))
        for n, (acc1, m1, den1) in zip(group, _attend_group(group_jobs(1, [(0, n) for n in group]), lo)):
            rows = pl.ds(n * nb, nb)
            m4, m16 = m4_s[rows, :], m16_s[rows, :]
            m = jnp.maximum(jnp.maximum(m1, m4), m16)
            w1, w4, w16 = jnp.exp2(m1 - m), jnp.exp2(m4 - m), jnp.exp2(m16 - m)
            num = w1 * acc1 + w4 * acc4_s[rows, :] + w16 * acc16_s[rows, :]
            den = w1 * den1 + w4 * den4_s[rows, :] + w16 * den16_s[rows, :]
            o_ref[rows, :] = (num / den).astype(o_ref.dtype)


def _attn_sample_kernel(q_ref, kn_ref, vn_ref, kt_ref, vt_ref, o_ref):
    T = q_ref.shape[0]
    lb = kt_ref.shape[1]
    (_, _), (w4, d4), (_, d16) = DILATED_PATTERNS
    na = w4
    q = q_ref[...]
    kn, vn = kn_ref[...], vn_ref[...]
    lane8 = lax.broadcasted_iota(jnp.int32, (N_ATT_HEADS, ATT_WIDTH), 1)
    head8 = lax.broadcasted_iota(jnp.int32, (N_ATT_HEADS, ATT_WIDTH), 0)
    own = (lane8 // HEAD_DIM) == head8
    qrows = jnp.concatenate(
        [jnp.where(own, jnp.broadcast_to(q[t:t + 1, :], own.shape), 0.0) for t in range(T)], axis=0)
    nr = qrows.shape[0]
    qb = qrows.astype(BF16)
    trow = lax.broadcasted_iota(jnp.int32, (nr, 1), 0) // N_ATT_HEADS
    s_all = jnp.dot(qb, kt_ref[...].astype(BF16), preferred_element_type=F32)
    s_a = s_all[:, lb - na:]
    vt = vt_ref[...].astype(BF16)
    knb = kn.astype(BF16).astype(F32)
    s_new = [jnp.sum(qb.astype(F32) * knb[t:t + 1, :], axis=-1, keepdims=True) for t in range(T)]
    col = lax.broadcasted_iota(jnp.int32, (nr, na), 1)
    trow_b = jnp.broadcast_to(trow, (nr, na))
    vnb = vn.astype(BF16).astype(F32)

    def partial_softmax(s_cache, new_ok):
        m = jnp.max(s_cache, axis=-1, keepdims=True)
        for t in range(T):
            m = jnp.maximum(m, jnp.where(new_ok(t), s_new[t], -jnp.inf))
        p = jnp.exp2(s_cache - m)
        den = jnp.sum(p, axis=-1, keepdims=True)
        p_new = [jnp.where(new_ok(t), jnp.exp2(s_new[t] - m), 0.0) for t in range(T)]
        return p.astype(BF16), p_new, m, den + sum(p_new)

    pos = lax.broadcasted_iota(jnp.int32, (nr, lb), 1)
    patterns = [
        partial_softmax(jnp.where(col >= na - N_BACK + trow_b, s_a, -jnp.inf), lambda t: trow >= t),
        partial_softmax(jnp.where((col % d4) == trow_b, s_a, -jnp.inf), lambda t: trow == t),
        partial_softmax(jnp.where((pos % d16) == jnp.broadcast_to(trow, (nr, lb)), s_all, -jnp.inf),
                        lambda t: trow == t),
    ]
    widen = lambda p: p if p.shape[1] == lb else jnp.concatenate([jnp.zeros((nr, lb - na), BF16), p], axis=1)
    p_all = jnp.concatenate([widen(p) for p, _, _, _ in patterns], axis=0)
    acc_all = lax.dot_general(p_all, vt, _NT, preferred_element_type=F32)
    m = jnp.maximum(jnp.maximum(patterns[0][2], patterns[1][2]), patterns[2][2])
    num, den = 0.0, 0.0
    for i, (_, p_new, m_i, den_i) in enumerate(patterns):
        acc = acc_all[i * nr:(i + 1) * nr]
        for t in range(T):
            acc = acc + p_new[t].astype(BF16).astype(F32) * vnb[t:t + 1, :]
        w = jnp.exp2(m_i - m)
        num, den = num + w * acc, den + w * den_i
    merged = num / den
    for t in range(T):
        rows = slice(t * N_ATT_HEADS, (t + 1) * N_ATT_HEADS)
        o_ref[t:t + 1, :] = jnp.sum(jnp.where(own, merged[rows], 0.0), axis=0, keepdims=True).astype(o_ref.dtype)


def _attn_both_kernel(q_ref, k_ref, v_ref, qs_ref, kn_ref, vn_ref, kt_ref, vt_ref, o_ref, os_ref, *scratch):
    _attn_sample_kernel(qs_ref, kn_ref, vn_ref, kt_ref, vt_ref, os_ref)
    _attn_prompt_kernel(q_ref, k_ref, v_ref, o_ref, *scratch)


def _attention(q, k, v, qs, k_new, v_new, cache_kt, cache_vt):
    batch, seq, _ = q.shape
    bd, T, _ = qs.shape
    lb = cache_kt.shape[2]
    (_, _), (w4, d4), (w16, d16) = DILATED_PATTERNS
    for window, dil in DILATED_PATTERNS:
        assert window // dil == N_BACK and seq % (N_BACK * dil) == 0
    assert lb == w16 and lb % d16 == 0 and w4 % d4 == 0 and T <= d4
    params = pltpu.CompilerParams(dimension_semantics=("arbitrary", "arbitrary"), vmem_limit_bytes=VMEM_LIMIT)
    pair = pl.BlockSpec((None, seq, LANES), lambda b, p: (b, 0, p))
    prompt_shape = jax.ShapeDtypeStruct((batch, seq, ATT_WIDTH), BF16)
    prompt_scratch = [pltpu.VMEM((seq, LANES), F32)] * 6

    def sample_specs(index):
        new = pl.BlockSpec((None, T, ATT_WIDTH), lambda b, p: (index(b, p), 0, 0))
        cache = pl.BlockSpec((None, ATT_WIDTH, lb), lambda b, p: (index(b, p), 0, 0))
        return [new, new, new, cache, cache], new

    sample_shape = jax.ShapeDtypeStruct((bd, T, ATT_WIDTH), BF16)
    if bd == batch * N_PAIRS:
        in_s, out_s = sample_specs(lambda b, p: b * N_PAIRS + p)
        return pl.pallas_call(
            _attn_both_kernel, grid=(batch, N_PAIRS), in_specs=[pair, pair, pair] + in_s, out_specs=[pair, out_s],
            out_shape=[prompt_shape, sample_shape], scratch_shapes=prompt_scratch, compiler_params=params,
            name="attention",
        )(q, k, v, qs, k_new, v_new, cache_kt, cache_vt)
    att = pl.pallas_call(
        _attn_prompt_kernel, grid=(batch, N_PAIRS), in_specs=[pair, pair, pair], out_specs=pair,
        out_shape=prompt_shape, scratch_shapes=prompt_scratch, compiler_params=params, name="attn_prompt",
    )(q, k, v)
    in_s, out_s = sample_specs(lambda b, p: b)
    att_s = pl.pallas_call(
        _attn_sample_kernel, grid=(bd, 1), in_specs=in_s, out_specs=out_s, out_shape=sample_shape,
        compiler_params=params, name="attn_sample",
    )(qs, k_new, v_new, cache_kt, cache_vt)
    return att, att_s


def _mlstm_kernel(q_ref, k_ref, v_ref, mo_ref, g_ref, gain_ref, c0_ref, n0_ref, m0_ref,
                  hm_ref, c_ref, n_ref, m_ref, nt_s, num_s, kvn_s, col_s, row_s, *, chunk):
    L = chunk
    heads = range(N_ML_HEADS)
    c_ref[...] = c0_ref[...]
    m_ref[...] = m0_ref[...]
    eye = (lax.broadcasted_iota(jnp.int32, (ML_DK, LANES), 0) == lax.broadcasted_iota(jnp.int32, (ML_DK, LANES), 1))
    for h in heads:
        n_col = jnp.sum(jnp.where(eye, jnp.broadcast_to(n0_ref[h], (ML_DK, LANES)), 0.0), axis=-1, keepdims=True)
        nt_s[h] = jnp.broadcast_to(n_col, (ML_DK, LANES))
    lane = lax.broadcasted_iota(jnp.int32, (1, LANES), 1)
    sel_ig = [(lane == h).astype(F32) for h in heads]
    sel_lf = [(lane == h + N_ML_HEADS).astype(F32) for h in heads]
    row = lax.broadcasted_iota(jnp.int32, (L, L), 0)
    col = lax.broadcasted_iota(jnp.int32, (L, L), 1)
    tril = col <= row
    tril_b = tril.astype(F32).astype(BF16)
    triu_b = (row <= col).astype(F32).astype(BF16)
    eye8_b = (lax.broadcasted_iota(jnp.int32, (8, LANES), 0)
              == lax.broadcasted_iota(jnp.int32, (8, LANES), 1)).astype(F32).astype(BF16)

    def local_part(c):
        rows = pl.ds(pl.multiple_of(c * L, L), L)
        g = g_ref[rows, :]
        g_parts = _split3(g)
        b_all = sum(jnp.dot(tril_b, p, preferred_element_type=F32) for p in g_parts)
        if L % LANES == 0:
            g_rows = g.T[:8]
            b_rows = sum(jnp.dot(p, triu_b, preferred_element_type=F32) for p in _split3(g_rows))
        else:
            g_rows = sum(lax.dot_general(eye8_b, p, _NT, preferred_element_type=F32) for p in g_parts)
            b_rows = sum(lax.dot_general(eye8_b, p, _NT, preferred_element_type=F32) for p in _split3(b_all))
        q = [q_ref[h, rows, :] for h in heads]
        k = [k_ref[h, rows, :] for h in heads]
        v = [v_ref[h, rows, :] for h in heads]
        total = jnp.sum(g, axis=0, keepdims=True)
        back = total - b_all
        bcol = [jnp.sum(b_all * sel_lf[h], axis=-1, keepdims=True) for h in heads]
        gcol = [jnp.sum(back * sel_lf[h] + g * sel_ig[h], axis=-1, keepdims=True) for h in heads]
        b_last = [jnp.sum(total * sel_lf[h], axis=-1, keepdims=True) for h in heads]
        d = [jnp.where(tril, bcol[h] - b_rows[N_ML_HEADS + h:N_ML_HEADS + h + 1, :] + g_rows[h:h + 1, :], -jnp.inf)
             for h in heads]
        m_in = [jnp.max(d[h], axis=-1, keepdims=True) for h in heads]
        qk = [lax.dot_general(q[h], k[h], _NT, preferred_element_type=F32) for h in heads]
        a = [jnp.exp(d[h] - m_in[h]) * qk[h] for h in heads]
        num_in = [jnp.dot(a[h].astype(BF16), v[h], preferred_element_type=F32) for h in heads]
        den_in = [jnp.sum(a[h], axis=-1, keepdims=True) for h in heads]
        g_tile = [jnp.broadcast_to(gcol[h], (L, LANES)) for h in heads]
        g_max = [jnp.max(g_tile[h], axis=0, keepdims=True) for h in heads]
        w_s = [jnp.exp(g_tile[h] - g_max[h]) for h in heads]
        kvn = [lax.dot_general(
            k[h], jnp.concatenate([(w_s[h] * v[h].astype(F32)).astype(BF16), w_s[h].astype(BF16)], axis=1),
            _TN, preferred_element_type=F32) for h in heads]
        slot = c % 2
        for h in heads:
            num_s[slot, h] = num_in[h]
            kvn_s[slot, h] = kvn[h]
            col_s[slot, h, 0] = jnp.broadcast_to(den_in[h], (L, LANES))
            col_s[slot, h, 1] = jnp.broadcast_to(m_in[h], (L, LANES))
            col_s[slot, h, 2] = jnp.broadcast_to(bcol[h], (L, LANES))
            row_s[slot, h, 0:1, :] = g_max[h]
            row_s[slot, h, 1:2, :] = jnp.broadcast_to(b_last[h], (1, LANES))

    def staged(c):
        slot = c % 2
        return [(num_s[slot, h], kvn_s[slot, h], col_s[slot, h, 0], col_s[slot, h, 1], col_s[slot, h, 2],
                 row_s[slot, h, 0:1, :], row_s[slot, h, 1:2, :]) for h in heads]

    def state_part(c, stage):
        rows = pl.ds(pl.multiple_of(c * L, L), L)
        q = [q_ref[h, rows, :] for h in heads]
        m_prev = [m_ref[h] for h in heads]
        cmat = [c_ref[h] for h in heads]
        ntile = [nt_s[h] for h in heads]
        q_cn = [jnp.dot(q[h], jnp.concatenate([cmat[h].astype(BF16), ntile[h].astype(BF16)], axis=1),
                        preferred_element_type=F32) for h in heads]
        for h in heads:
            num_in, kvn, den_in, m_in, bcol, g_max, b_last = stage[h]
            inter = bcol + m_prev[h]
            m_t = jnp.maximum(m_in, inter)
            f_in, f_st = jnp.exp(m_in - m_t), jnp.exp(inter - m_t)
            num = f_in * num_in + f_st * q_cn[h][:, :ML_DK]
            den = f_in * den_in + f_st * q_cn[h][:, ML_DK:]
            den = jnp.maximum(jnp.abs(den), jnp.exp(-m_t))
            y = _rms(num / den, gain_ref[h]) * jax.nn.sigmoid(mo_ref[h, rows, :])
            hm_ref[rows, h * ML_DK:(h + 1) * ML_DK] = y.astype(hm_ref.dtype)
            m_new = jnp.maximum(b_last + m_prev[h], g_max)
            w_c = jnp.exp(b_last + m_prev[h] - m_new)
            w_k = jnp.exp(g_max - m_new)
            c_ref[h] = w_c * cmat[h] + w_k * kvn[:, :ML_DK]
            nt_s[h] = w_c * ntile[h] + w_k * kvn[:, ML_DK:]
            m_ref[h] = m_new

    def body(c, carry):
        state_part(c, staged(c))
        local_part(c + 1)
        return carry

    n_chunks = q_ref.shape[1] // L
    local_part(0)
    lax.fori_loop(0, n_chunks - 1, body, 0, unroll=3 if (n_chunks - 1) % 3 == 0 else 1)
    state_part(n_chunks - 1, staged(n_chunks - 1))
    for h in heads:
        n_ref[h] = jnp.sum(jnp.where(eye, nt_s[h], 0.0), axis=0, keepdims=True)


def _mlstm(mq, mk, mv, mo, gates, gain, c0, n0, m0, chunk):
    nh, batch, seq, _ = mq.shape
    per_head = pl.BlockSpec((nh, None, seq, LANES), lambda b: (0, b, 0, 0))
    state = lambda r: pl.BlockSpec((None, nh, r, LANES), lambda b: (b, 0, 0, 0))
    return pl.pallas_call(
        functools.partial(_mlstm_kernel, chunk=chunk),
        grid=(batch,),
        in_specs=[per_head, per_head, per_head, per_head,
                  pl.BlockSpec((None, seq, LANES), lambda b: (b, 0, 0)),
                  pl.BlockSpec((nh, 1, LANES), lambda b: (0, 0, 0)),
                  state(ML_DK), state(1), state(1)],
        out_specs=[pl.BlockSpec((None, seq, ML_WIDTH), lambda b: (b, 0, 0)), state(ML_DK), state(1), state(1)],
        out_shape=[jax.ShapeDtypeStruct((batch, seq, ML_WIDTH), BF16),
                   jax.ShapeDtypeStruct((batch, nh, ML_DK, LANES), F32),
                   jax.ShapeDtypeStruct((batch, nh, 1, LANES), F32),
                   jax.ShapeDtypeStruct((batch, nh, 1, LANES), F32)],
        scratch_shapes=[pltpu.VMEM((nh, ML_DK, LANES), F32), pltpu.VMEM((2, nh, chunk, LANES), F32),
                        pltpu.VMEM((2, nh, ML_DK, 2 * LANES), F32), pltpu.VMEM((2, nh, 3, chunk, LANES), F32),
                        pltpu.VMEM((2, nh, 8, LANES), F32)],
        compiler_params=pltpu.CompilerParams(dimension_semantics=("arbitrary",), vmem_limit_bytes=VMEM_LIMIT),
        name="mlstm",
    )(mq, mk, mv, mo, gates, gain, c0, n0, m0)


def _rope_tables(pos):
    half = HEAD_DIM // 2
    inv = ROPE_THETA ** (-jnp.arange(half, dtype=F32) / half)
    ang = pos.astype(F32)[:, None] * inv[None, :]
    cos, sin = jnp.cos(ang), jnp.sin(ang)
    return jnp.tile(jnp.concatenate([cos, cos], axis=-1), (1, 2)), jnp.tile(jnp.concatenate([-sin, sin], axis=-1), (1, 2))


def _layer_weights(l, ffn1_norm, ffn1_w_gate, ffn1_w_up, ffn1_w_down, mix_norm, w_in, q_norm, k_norm, b_igate,
                   b_fgate, ml_out_norm, w_out, ffn2_norm, ffn2_w_gate, ffn2_w_up, ffn2_w_down):
    pad_lanes = lambda a: jnp.pad(a, ((0, 0), (0, LANES - a.shape[1])))
    return {
        "g1": ffn1_norm[l][None, :], "wg1": ffn1_w_gate[l].astype(BF16), "wu1": ffn1_w_up[l].astype(BF16),
        "wd1": ffn1_w_down[l].astype(BF16), "gm": mix_norm[l][None, :],
        "win": w_in[l][:, :MAIN_WIDTH].astype(BF16), "wgate": pad_lanes(w_in[l][:, MAIN_WIDTH:]).astype(BF16),
        "bgate": pad_lanes(jnp.concatenate([b_igate[l], b_fgate[l]])[None, :]),
        "qg": jnp.tile(q_norm[l], 2)[None, :], "kg": jnp.tile(k_norm[l], 2)[None, :],
        "gain": ml_out_norm[l][:, None, :], "wo": w_out[l].astype(BF16),
        "g2": ffn2_norm[l][None, :], "wg2": ffn2_w_gate[l].astype(BF16), "wu2": ffn2_w_up[l].astype(BF16),
        "wd2": ffn2_w_down[l].astype(BF16),
    }


def _layer(xp, xs, w, cache_k, cache_v, c0, n0, m0):
    batch, seq, _ = xp.shape
    bd, T, _ = xs.shape
    n, ns = batch * seq, bd * T
    cos, sin = _rope_tables(jnp.arange(seq, dtype=jnp.int32))
    x1, q, k, v, kt, vt, mq, mk, mv, mo, gates = _front(xp.reshape(n, D_MODEL), cos, sin, w, seq)
    cos, sin = _rope_tables(PAST_LEN + jnp.arange(T, dtype=jnp.int32))
    x1s, qs, ks, vs, _, _, mqs, mks, mvs, mos, gates_s = _front(
        xs.reshape(ns, D_MODEL), jnp.tile(cos, (bd, 1)), jnp.tile(sin, (bd, 1)), w, ns)

    seqs = lambda t: t.reshape(batch, seq, ATT_WIDTH)
    news = lambda t: t.reshape(bd, T, ATT_WIDTH)
    transposed = lambda c: c.transpose(0, 2, 3, 1).reshape(bd, ATT_WIDTH, c.shape[1])
    att, att_s = _attention(seqs(q), seqs(k), seqs(v), news(qs), news(ks), news(vs),
                            transposed(cache_k), transposed(cache_v))

    heads = lambda t: t.reshape(N_ML_HEADS, batch, seq, LANES)
    zeros = lambda r: jnp.zeros((batch, N_ML_HEADS, r, LANES), F32)
    hm, c, nv, m = _mlstm(heads(mq), heads(mk), heads(mv), heads(mo), gates.reshape(batch, seq, LANES), w["gain"],
                          zeros(ML_DK), zeros(1), zeros(1), min(ML_CHUNK, seq))
    tp = 8
    heads_s = lambda t: jnp.pad(t.reshape(N_ML_HEADS, bd, T, LANES), ((0, 0), (0, 0), (0, tp - T), (0, 0)))
    pad_gate = jnp.where(jnp.arange(LANES) < N_ML_HEADS, NEG_BIG, 0.0).astype(F32)
    gates_s = jnp.concatenate([gates_s.reshape(bd, T, LANES), jnp.broadcast_to(pad_gate, (bd, tp - T, LANES))], axis=1)
    hm_s, c_s, nv_s, m_s = _mlstm(heads_s(mqs), heads_s(mks), heads_s(mvs), heads_s(mos), gates_s, w["gain"], c0,
                                  n0[:, :, None, :], jnp.broadcast_to(m0[:, :, None, None], (bd, N_ML_HEADS, 1, LANES)),
                                  tp)

    y = _back(x1, att.reshape(n, ATT_WIDTH), hm.reshape(n, ML_WIDTH), w)
    y_s = _back(x1s, att_s.reshape(ns, ATT_WIDTH), hm_s[:, :T].reshape(ns, ML_WIDTH), w)
    kv = lambda t: t.reshape(batch, N_ATT_HEADS, HEAD_DIM, seq).transpose(0, 3, 1, 2)
    kv_s = lambda t: t.reshape(bd, T, N_ATT_HEADS, HEAD_DIM)
    return (y.reshape(batch, seq, D_MODEL), y_s.reshape(bd, T, D_MODEL),
            (kv(kt), kv(vt), kv_s(ks), kv_s(vs), c, nv[:, :, 0, :], m[:, :, 0, 0],
             c_s, nv_s[:, :, 0, :], m_s[:, :, 0, 0]))


def kernel(x_prompt, x_sample, cache_k_win, cache_v_win, state_C, state_n, state_m, ffn1_norm, ffn1_w_gate,
           ffn1_w_up, ffn1_w_down, mix_norm, w_in, q_norm, k_norm, b_igate, b_fgate, ml_out_norm, w_out,
           ffn2_norm, ffn2_w_gate, ffn2_w_up, ffn2_w_down):
    depth = w_in.shape[0]
    yp, ys = x_prompt, x_sample
    outs = [[] for _ in range(10)]
    for l in range(depth):
        w = _layer_weights(l, ffn1_norm, ffn1_w_gate, ffn1_w_up, ffn1_w_down, mix_norm, w_in, q_norm, k_norm,
                           b_igate, b_fgate, ml_out_norm, w_out, ffn2_norm, ffn2_w_gate, ffn2_w_up, ffn2_w_down)
        yp, ys, states = _layer(yp, ys, w, cache_k_win[l], cache_v_win[l], state_C[l], state_n[l], state_m[l])
        for acc, val in zip(outs, states):
            acc.append(val)
    return (yp, ys) + tuple(jnp.stack(o) for o in outs)
```

```python
import functools

import jax
import jax.numpy as jnp
from jax import lax
from jax.experimental import pallas as pl
from jax.experimental.pallas import tpu as pltpu

F32 = jnp.float32
BF16 = jnp.bfloat16

D_MODEL = 1024
HEAD_DIM = 64
N_ATT_HEADS = 8
ATT_WIDTH = N_ATT_HEADS * HEAD_DIM
N_ML_HEADS = 4
ML_DK = 128
ML_WIDTH = N_ML_HEADS * ML_DK
D_FF = 2816
ML_CHUNK = 128
NORM_EPS = 1e-6
ROPE_THETA = 10000.0
PAST_LEN = 8192
DILATED_PATTERNS = ((128, 1), (512, 4), (2048, 16))
N_BACK = 128
ATT_SCALE = HEAD_DIM ** -0.5
LOG2_E = 1.4426950408889634
Q_SCALE = ATT_SCALE * LOG2_E
MAIN_WIDTH = 3 * ATT_WIDTH + 4 * ML_WIDTH

LANES = 128
N_PAIRS = ATT_WIDTH // LANES
ROW_TILE = 256
FRONT_TILE = 2 * ROW_TILE
BACK_TILE = 2 * ROW_TILE
FF_CHUNK = 256
ML_SAMPLE_BLOCK = 8
ATT_GROUP = 8
VMEM_LIMIT = 56 * 1024 * 1024
NEG_BIG = -1e30

_NT = (((1,), (1,)), ((), ()))
_TN = (((0,), (0,)), ((), ()))


def _split3(x):
    hi = x.astype(BF16)
    rest = x - hi.astype(F32)
    mid = rest.astype(BF16)
    return hi, mid, (rest - mid.astype(F32)).astype(BF16)


def _rms(x, g):
    return x * lax.rsqrt(jnp.mean(x * x, axis=-1, keepdims=True) + NORM_EPS) * g


def _swiglu(hn, wg_ref, wu_ref, wd_ref, act_ref, rows=slice(None)):
    for c in range(D_FF // FF_CHUNK):
        sl = slice(c * FF_CHUNK, (c + 1) * FF_CHUNK)
        g = jnp.dot(hn, wg_ref[:, sl], preferred_element_type=F32)
        u = jnp.dot(hn, wu_ref[:, sl], preferred_element_type=F32)
        act_ref[rows, sl] = (g * jax.nn.sigmoid(g) * u).astype(BF16)
    return jnp.dot(act_ref[rows, :], wd_ref[...], preferred_element_type=F32)


def _pair_norm_rope(x, gain, cos, sin, lo):
    sq = x * x
    s_lo = jnp.sum(jnp.where(lo, sq, 0.0), axis=-1, keepdims=True)
    s_hi = jnp.sum(jnp.where(lo, 0.0, sq), axis=-1, keepdims=True)
    ms = jnp.where(lo, s_lo, s_hi) * (1.0 / HEAD_DIM)
    y = x * lax.rsqrt(ms + NORM_EPS) * gain
    lane = lax.broadcasted_iota(jnp.int32, y.shape, 1)
    first_half = (lane % HEAD_DIM) < (HEAD_DIM // 2)
    partner = jnp.where(first_half, pltpu.roll(y, LANES - HEAD_DIM // 2, 1), pltpu.roll(y, HEAD_DIM // 2, 1))
    return y * cos + partner * sin


def _front_kernel(x_ref, cos_ref, sin_ref, g1_ref, wg_ref, wu_ref, wd_ref, gm_ref, win_ref, wgate_ref,
                  bgate_ref, qg_ref, kg_ref,
                  x1_ref, q_ref, k_ref, v_ref, kt_ref, vt_ref, mq_ref, mk_ref, mv_ref, mo_ref, gates_ref,
                  act_ref):
    lane = lax.broadcasted_iota(jnp.int32, (ROW_TILE, LANES), 1)
    lo = lane < HEAD_DIM
    pairs = [slice(p * LANES, (p + 1) * LANES) for p in range(N_PAIRS)]
    for r in range(x_ref.shape[0] // ROW_TILE):
        rows = slice(r * ROW_TILE, (r + 1) * ROW_TILE)
        x = x_ref[rows, :]
        hn = _rms(x, g1_ref[...]).astype(BF16)
        x1 = x + 0.5 * _swiglu(hn, wg_ref, wu_ref, wd_ref, act_ref, rows)
        x1_ref[rows, :] = x1
        h = _rms(x1, gm_ref[...]).astype(BF16)

        def proj(i):
            return jnp.dot(h, win_ref[:, i * ATT_WIDTH:(i + 1) * ATT_WIDTH], preferred_element_type=F32)

        cos, sin = cos_ref[rows, :], sin_ref[rows, :]
        aq, ak, av = proj(0), proj(1), proj(2)
        for sl in pairs:
            q_ref[rows, sl] = _pair_norm_rope(aq[:, sl], qg_ref[...], cos, sin, lo) * Q_SCALE
        ak = jnp.concatenate([_pair_norm_rope(ak[:, sl], kg_ref[...], cos, sin, lo) for sl in pairs], axis=1)
        k_ref[rows, :] = ak
        v_ref[rows, :] = av
        kt_ref[:, rows] = ak.T
        vt_ref[:, rows] = av.T
        mq, mk, mv, mo = proj(3), proj(4), proj(5), proj(6)
        for p in range(N_ML_HEADS):
            sl = slice(p * ML_DK, (p + 1) * ML_DK)
            mq_ref[p, rows, :] = mq[:, sl].astype(BF16)
            mk_ref[p, rows, :] = (mk[:, sl] * (ML_DK ** -0.5)).astype(BF16)
            mv_ref[p, rows, :] = mv[:, sl].astype(BF16)
            mo_ref[p, rows, :] = mo[:, sl]
        zg = jnp.dot(h, wgate_ref[...], preferred_element_type=F32) + bgate_ref[...]
        log_sig = jnp.minimum(zg, 0.0) - jnp.log(1.0 + jnp.exp(-jnp.abs(zg)))
        gates_ref[rows, :] = jnp.where(lane < N_ML_HEADS, zg, log_sig)


def _back_kernel(x1_ref, att_ref, hm_ref, wo_ref, g2_ref, wg_ref, wu_ref, wd_ref, y_ref, act_ref):
    for r in range(x1_ref.shape[0] // ROW_TILE):
        rows = slice(r * ROW_TILE, (r + 1) * ROW_TILE)
        x2 = (x1_ref[rows, :]
              + jnp.dot(att_ref[rows, :], wo_ref[:ATT_WIDTH, :], preferred_element_type=F32)
              + jnp.dot(hm_ref[rows, :], wo_ref[ATT_WIDTH:, :], preferred_element_type=F32))
        hn = _rms(x2, g2_ref[...]).astype(BF16)
        y_ref[rows, :] = x2 + 0.5 * _swiglu(hn, wg_ref, wu_ref, wd_ref, act_ref, rows)


def _const_spec(shape):
    return pl.BlockSpec(shape, lambda i: (0,) * len(shape), pipeline_mode=pl.Buffered(1))


def _dense_params():
    return pltpu.CompilerParams(dimension_semantics=("arbitrary",), vmem_limit_bytes=VMEM_LIMIT)


def _front(x, cos, sin, w, seq):
    n = x.shape[0]
    tm = FRONT_TILE
    blocks = seq // tm
    row = lambda width: pl.BlockSpec((tm, width), lambda i: (i, 0))
    rope = pl.BlockSpec((tm, LANES), lambda i: (i % blocks, 0))
    grouped = lambda g: pl.BlockSpec((g, tm, LANES), lambda i: (0, i, 0))
    gshape = lambda g, dt: jax.ShapeDtypeStruct((g, n, LANES), dt)
    transposed = pl.BlockSpec((None, ATT_WIDTH, tm), lambda i: (i // blocks, 0, i % blocks))
    tshape = jax.ShapeDtypeStruct((n // seq, ATT_WIDTH, seq), F32)
    return pl.pallas_call(
        _front_kernel,
        grid=(n // tm,),
        in_specs=[row(D_MODEL), rope, rope,
                  _const_spec((1, D_MODEL)), _const_spec((D_MODEL, D_FF)), _const_spec((D_MODEL, D_FF)),
                  _const_spec((D_FF, D_MODEL)), _const_spec((1, D_MODEL)), _const_spec((D_MODEL, MAIN_WIDTH)),
                  _const_spec((D_MODEL, LANES)), _const_spec((1, LANES)), _const_spec((1, LANES)),
                  _const_spec((1, LANES))],
        out_specs=[row(D_MODEL), row(ATT_WIDTH), row(ATT_WIDTH), row(ATT_WIDTH), transposed, transposed,
                   grouped(N_ML_HEADS), grouped(N_ML_HEADS), grouped(N_ML_HEADS), grouped(N_ML_HEADS), row(LANES)],
        out_shape=[jax.ShapeDtypeStruct((n, D_MODEL), F32), jax.ShapeDtypeStruct((n, ATT_WIDTH), F32),
                   jax.ShapeDtypeStruct((n, ATT_WIDTH), F32), jax.ShapeDtypeStruct((n, ATT_WIDTH), F32),
                   tshape, tshape, gshape(N_ML_HEADS, BF16), gshape(N_ML_HEADS, BF16), gshape(N_ML_HEADS, BF16),
                   gshape(N_ML_HEADS, F32), jax.ShapeDtypeStruct((n, LANES), F32)],
        scratch_shapes=[pltpu.VMEM((tm, D_FF), BF16)],
        compiler_params=_dense_params(),
        name="front",
    )(x, cos, sin, w["g1"], w["wg1"], w["wu1"], w["wd1"], w["gm"], w["win"], w["wgate"], w["bgate"],
      w["qg"], w["kg"])


def _back(x1, att, hm, w):
    n = x1.shape[0]
    tm = BACK_TILE
    row = lambda width: pl.BlockSpec((tm, width), lambda i: (i, 0))
    return pl.pallas_call(
        _back_kernel,
        grid=(n // tm,),
        in_specs=[row(D_MODEL), row(ATT_WIDTH), row(ML_WIDTH), _const_spec((D_MODEL, D_MODEL)),
                  _const_spec((1, D_MODEL)), _const_spec((D_MODEL, D_FF)), _const_spec((D_MODEL, D_FF)),
                  _const_spec((D_FF, D_MODEL))],
        out_specs=row(D_MODEL),
        out_shape=jax.ShapeDtypeStruct((n, D_MODEL), F32),
        scratch_shapes=[pltpu.VMEM((tm, D_FF), BF16)],
        compiler_params=_dense_params(),
        name="back",
    )(x1, att, hm, w["wo"], w["g2"], w["wg2"], w["wu2"], w["wd2"])


def _attend_group(jobs, lo):
    zero = jnp.zeros((), BF16)
    one = jnp.ones((), BF16)
    scores, values = [], []
    for q, k, v, _ in jobs:
        lo_k = lax.broadcasted_iota(jnp.int32, k.shape, 1) < HEAD_DIM
        k2 = jnp.concatenate([jnp.where(lo_k, k, zero), jnp.where(lo_k, zero, k)], axis=0)
        scores.append(lax.dot_general(q, k2, _NT, preferred_element_type=F32))
        ones_lo = lo_k.astype(F32).astype(BF16)
        ones_hi = one - ones_lo
        values.append(jnp.concatenate(
            [jnp.concatenate([jnp.where(lo_k, v, zero), ones_lo], axis=1),
             jnp.concatenate([jnp.where(lo_k, zero, v), ones_hi], axis=1)], axis=0))
    masked = []
    for (_, k, _, mask), s in zip(jobs, scores):
        nk = k.shape[0]
        masked.append([jnp.where(mask, s[:, h * nk:(h + 1) * nk].astype(BF16), jnp.asarray(-jnp.inf, BF16))
                       for h in range(2)])
    maxes = [[jnp.max(s, axis=-1, keepdims=True) for s in pair] for pair in masked]
    probs = [jnp.concatenate([jnp.exp2(s - m) for s, m in zip(pair, mx)], axis=1)
             for pair, mx in zip(masked, maxes)]
    sums = [jnp.dot(p, v2, preferred_element_type=F32) for p, v2 in zip(probs, values)]
    return [(r[:, :LANES], jnp.where(lo, mx[0].astype(F32), mx[1].astype(F32)), r[:, LANES:])
            for r, mx in zip(sums, maxes)]


def _attn_prompt_kernel(q_ref, k_ref, v_ref, o_ref, acc4_s, m4_s, den4_s, acc16_s, m16_s, den16_s):
    nb = N_BACK
    seq = q_ref.shape[0]
    lo = lax.broadcasted_iota(jnp.int32, (nb, LANES), 1) < HEAD_DIM
    r2 = lax.broadcasted_iota(jnp.int32, (nb, 2 * nb), 0)
    c2 = lax.broadcasted_iota(jnp.int32, (nb, 2 * nb), 1)
    band = (c2 >= r2) & (c2 <= r2 + nb)
    causal = (lax.broadcasted_iota(jnp.int32, (nb, nb), 1)
              <= lax.broadcasted_iota(jnp.int32, (nb, nb), 0))

    def group_jobs(dil, group):
        loaded = {}

        def load(ref, r, blk):
            key = (id(ref), r, blk)
            if key not in loaded:
                rows = pl.ds(r + blk * nb * dil, nb, stride=dil) if dil > 1 else pl.ds(blk * nb, nb)
                loaded[key] = ref[rows, :].astype(BF16)
            return loaded[key]

        jobs = []
        for r, n in group:
            q = load(q_ref, r, n)
            if n == 0:
                jobs.append((q, load(k_ref, r, 0), load(v_ref, r, 0), causal))
            else:
                jobs.append((q, jnp.concatenate([load(k_ref, r, n - 1), load(k_ref, r, n)], axis=0),
                             jnp.concatenate([load(v_ref, r, n - 1), load(v_ref, r, n)], axis=0), band))
        return jobs

    def dilated(dil, acc_s, m_s, den_s):
        blocks = [(r, n) for r in range(dil) for n in range(seq // (nb * dil))]
        for g in range(0, len(blocks), ATT_GROUP):
            group = blocks[g:g + ATT_GROUP]
            for (r, n), (acc, m, den) in zip(group, _attend_group(group_jobs(dil, group), lo)):
                rows = pl.ds(r + n * nb * dil, nb, stride=dil)
                acc_s[rows, :] = acc
                m_s[rows, :] = m
                den_s[rows, :] = den

    dilated(DILATED_PATTERNS[2][1], acc16_s, m16_s, den16_s)
    dilated(DILATED_PATTERNS[1][1], acc4_s, m4_s, den4_s)

    for g in range(0, seq // nb, ATT_GROUP):
        group = list(range(g, g + ATT_GROUP))
        for n, (acc1, m1, den1) in zip(group, _attend_group(group_jobs(1, [(0, n) for n in group]), lo)):
            rows = pl.ds(n * nb, nb)
            m4, m16 = m4_s[rows, :], m16_s[rows, :]
            m = jnp.maximum(jnp.maximum(m1, m4), m16)
            w1, w4, w16 = jnp.exp2(m1 - m), jnp.exp2(m4 - m), jnp.exp2(m16 - m)
            num = w1 * acc1 + w4 * acc4_s[rows, :] + w16 * acc16_s[rows, :]
            den = w1 * den1 + w4 * den4_s[rows, :] + w16 * den16_s[rows, :]
            o_ref[rows, :] = (num / den).astype(o_ref.dtype)


def _attn_sample_kernel(q_ref, kn_ref, vn_ref, kt_ref, vt_ref, o_ref):
    T = q_ref.shape[0]
    lb = kt_ref.shape[1]
    (_, _), (w4, d4), (_, d16) = DILATED_PATTERNS
    na = w4
    q = q_ref[...]
    kn, vn = kn_ref[...], vn_ref[...]
    lane8 = lax.broadcasted_iota(jnp.int32, (N_ATT_HEADS, ATT_WIDTH), 1)
    head8 = lax.broadcasted_iota(jnp.int32, (N_ATT_HEADS, ATT_WIDTH), 0)
    own = (lane8 // HEAD_DIM) == head8
    qrows = jnp.concatenate(
        [jnp.where(own, jnp.broadcast_to(q[t:t + 1, :], own.shape), 0.0) for t in range(T)], axis=0)
    nr = qrows.shape[0]
    qb = qrows.astype(BF16)
    trow = lax.broadcasted_iota(jnp.int32, (nr, 1), 0) // N_ATT_HEADS
    s_all = jnp.dot(qb, kt_ref[...].astype(BF16), preferred_element_type=F32)
    s_a = s_all[:, lb - na:]
    vt = vt_ref[...].astype(BF16)
    knb = kn.astype(BF16).astype(F32)
    s_new = [jnp.sum(qb.astype(F32) * knb[t:t + 1, :], axis=-1, keepdims=True) for t in range(T)]
    col = lax.broadcasted_iota(jnp.int32, (nr, na), 1)
    trow_b = jnp.broadcast_to(trow, (nr, na))
    vnb = vn.astype(BF16).astype(F32)

    def partial_softmax(s_cache, new_ok):
        m = jnp.max(s_cache, axis=-1, keepdims=True)
        for t in range(T):
            m = jnp.maximum(m, jnp.where(new_ok(t), s_new[t], -jnp.inf))
        p = jnp.exp2(s_cache - m)
        den = jnp.sum(p, axis=-1, keepdims=True)
        p_new = [jnp.where(new_ok(t), jnp.exp2(s_new[t] - m), 0.0) for t in range(T)]
        return p.astype(BF16), p_new, m, den + sum(p_new)

    pos = lax.broadcasted_iota(jnp.int32, (nr, lb), 1)
    patterns = [
        partial_softmax(jnp.where(col >= na - N_BACK + trow_b, s_a, -jnp.inf), lambda t: trow >= t),
        partial_softmax(jnp.where((col % d4) == trow_b, s_a, -jnp.inf), lambda t: trow == t),
        partial_softmax(jnp.where((pos % d16) == jnp.broadcast_to(trow, (nr, lb)), s_all, -jnp.inf),
                        lambda t: trow == t),
    ]
    widen = lambda p: p if p.shape[1] == lb else jnp.concatenate([jnp.zeros((nr, lb - na), BF16), p], axis=1)
    p_all = jnp.concatenate([widen(p) for p, _, _, _ in patterns], axis=0)
    acc_all = lax.dot_general(p_all, vt, _NT, preferred_element_type=F32)
    m = jnp.maximum(jnp.maximum(patterns[0][2], patterns[1][2]), patterns[2][2])
    num, den = 0.0, 0.0
    for i, (_, p_new, m_i, den_i) in enumerate(patterns):
        acc = acc_all[i * nr:(i + 1) * nr]
        for t in range(T):
            acc = acc + p_new[t].astype(BF16).astype(F32) * vnb[t:t + 1, :]
        w = jnp.exp2(m_i - m)
        num, den = num + w * acc, den + w * den_i
    merged = num / den
    for t in range(T):
        rows = slice(t * N_ATT_HEADS, (t + 1) * N_ATT_HEADS)
        o_ref[t:t + 1, :] = jnp.sum(jnp.where(own, merged[rows], 0.0), axis=0, keepdims=True).astype(o_ref.dtype)


def _attn_both_kernel(q_ref, k_ref, v_ref, qs_ref, kn_ref, vn_ref, kt_ref, vt_ref, o_ref, os_ref, *scratch):
    _attn_sample_kernel(qs_ref, kn_ref, vn_ref, kt_ref, vt_ref, os_ref)
    _attn_prompt_kernel(q_ref, k_ref, v_ref, o_ref, *scratch)


def _attention(q, k, v, qs, k_new, v_new, cache_kt, cache_vt):
    batch, seq, _ = q.shape
    bd, T, _ = qs.shape
    lb = cache_kt.shape[2]
    (_, _), (w4, d4), (w16, d16) = DILATED_PATTERNS
    for window, dil in DILATED_PATTERNS:
        assert window // dil == N_BACK and seq % (N_BACK * dil) == 0
    assert lb == w16 and lb % d16 == 0 and w4 % d4 == 0 and T <= d4
    params = pltpu.CompilerParams(dimension_semantics=("arbitrary", "arbitrary"), vmem_limit_bytes=VMEM_LIMIT)
    pair = pl.BlockSpec((None, seq, LANES), lambda b, p: (b, 0, p))
    prompt_shape = jax.ShapeDtypeStruct((batch, seq, ATT_WIDTH), BF16)
    prompt_scratch = [pltpu.VMEM((seq, LANES), F32)] * 6

    def sample_specs(index):
        new = pl.BlockSpec((None, T, ATT_WIDTH), lambda b, p: (index(b, p), 0, 0))
        cache = pl.BlockSpec((None, ATT_WIDTH, lb), lambda b, p: (index(b, p), 0, 0))
        return [new, new, new, cache, cache], new

    sample_shape = jax.ShapeDtypeStruct((bd, T, ATT_WIDTH), BF16)
    if bd == batch * N_PAIRS:
        in_s, out_s = sample_specs(lambda b, p: b * N_PAIRS + p)
        return pl.pallas_call(
            _attn_both_kernel, grid=(batch, N_PAIRS), in_specs=[pair, pair, pair] + in_s, out_specs=[pair, out_s],
            out_shape=[prompt_shape, sample_shape], scratch_shapes=prompt_scratch, compiler_params=params,
            name="attention",
        )(q, k, v, qs, k_new, v_new, cache_kt, cache_vt)
    att = pl.pallas_call(
        _attn_prompt_kernel, grid=(batch, N_PAIRS), in_specs=[pair, pair, pair], out_specs=pair,
        out_shape=prompt_shape, scratch_shapes=prompt_scratch, compiler_params=params, name="attn_prompt",
    )(q, k, v)
    in_s, out_s = sample_specs(lambda b, p: b)
    att_s = pl.pallas_call(
        _attn_sample_kernel, grid=(bd, 1), in_specs=in_s, out_specs=out_s, out_shape=sample_shape,
        compiler_params=params, name="attn_sample",
    )(qs, k_new, v_new, cache_kt, cache_vt)
    return att, att_s


def _mlstm_kernel(q_ref, k_ref, v_ref, mo_ref, g_ref, gain_ref, c0_ref, n0_ref, m0_ref,
                  hm_ref, c_ref, n_ref, m_ref, nt_s, num_s, kvn_s, col_s, row_s, *, chunk):
    L = chunk
    seqs = range(g_ref.shape[0])
    units = [(s, hd) for s in seqs for hd in range(N_ML_HEADS)]
    heads = range(len(units))
    seq_of = [s for s, _ in units]
    head_of = [hd for _, hd in units]
    c_ref[...] = c0_ref[...]
    m_ref[...] = m0_ref[...]
    eye = (lax.broadcasted_iota(jnp.int32, (ML_DK, LANES), 0) == lax.broadcasted_iota(jnp.int32, (ML_DK, LANES), 1))
    for h in heads:
        n_col = jnp.sum(jnp.where(eye, jnp.broadcast_to(n0_ref[units[h]], (ML_DK, LANES)), 0.0),
                        axis=-1, keepdims=True)
        nt_s[h] = jnp.broadcast_to(n_col, (ML_DK, LANES))
    lane = lax.broadcasted_iota(jnp.int32, (1, LANES), 1)
    sel_ig = [(lane == head_of[h]).astype(F32) for h in heads]
    sel_lf = [(lane == head_of[h] + N_ML_HEADS).astype(F32) for h in heads]
    row = lax.broadcasted_iota(jnp.int32, (L, L), 0)
    col = lax.broadcasted_iota(jnp.int32, (L, L), 1)
    tril = col <= row
    tril_b = tril.astype(F32).astype(BF16)
    triu_b = (row <= col).astype(F32).astype(BF16)
    eye8_b = (lax.broadcasted_iota(jnp.int32, (8, LANES), 0)
              == lax.broadcasted_iota(jnp.int32, (8, LANES), 1)).astype(F32).astype(BF16)

    def local_part(c):
        rows = pl.ds(pl.multiple_of(c * L, L), L)
        gs = [g_ref[s, rows, :] for s in seqs]
        g_parts = [_split3(gs[s]) for s in seqs]
        b_alls = [sum(jnp.dot(tril_b, p, preferred_element_type=F32) for p in g_parts[s]) for s in seqs]
        if L % LANES == 0:
            g_rows = [gs[s].T[:8] for s in seqs]
            b_rows = [sum(jnp.dot(p, triu_b, preferred_element_type=F32) for p in _split3(g_rows[s])) for s in seqs]
        else:
            g_rows = [sum(lax.dot_general(eye8_b, p, _NT, preferred_element_type=F32) for p in g_parts[s]) for s in seqs]
            b_rows = [sum(lax.dot_general(eye8_b, p, _NT, preferred_element_type=F32) for p in _split3(b_alls[s]))
                      for s in seqs]
        q = [q_ref[head_of[h], seq_of[h], rows, :] for h in heads]
        k = [k_ref[head_of[h], seq_of[h], rows, :] for h in heads]
        v = [v_ref[head_of[h], seq_of[h], rows, :] for h in heads]
        totals = [jnp.sum(gs[s], axis=0, keepdims=True) for s in seqs]
        backs = [totals[s] - b_alls[s] for s in seqs]
        g, b_all, total, back = ([t[seq_of[h]] for h in heads] for t in (gs, b_alls, totals, backs))
        bcol = [jnp.sum(b_all[h] * sel_lf[h], axis=-1, keepdims=True) for h in heads]
        gcol = [jnp.sum(back[h] * sel_lf[h] + g[h] * sel_ig[h], axis=-1, keepdims=True) for h in heads]
        b_last = [jnp.sum(total[h] * sel_lf[h], axis=-1, keepdims=True) for h in heads]
        d = [jnp.where(tril,
                       bcol[h] - b_rows[seq_of[h]][N_ML_HEADS + head_of[h]:N_ML_HEADS + head_of[h] + 1, :]
                       + g_rows[seq_of[h]][head_of[h]:head_of[h] + 1, :], -jnp.inf)
             for h in heads]
        m_in = [jnp.max(d[h], axis=-1, keepdims=True) for h in heads]
        qk = [lax.dot_general(q[h], k[h], _NT, preferred_element_type=F32) for h in heads]
        a = [jnp.exp(d[h] - m_in[h]) * qk[h] for h in heads]
        num_in = [jnp.dot(a[h].astype(BF16), v[h], preferred_element_type=F32) for h in heads]
        den_in = [jnp.sum(a[h], axis=-1, keepdims=True) for h in heads]
        g_tile = [jnp.broadcast_to(gcol[h], (L, LANES)) for h in heads]
        g_max = [jnp.max(g_tile[h], axis=0, keepdims=True) for h in heads]
        w_s = [jnp.exp(g_tile[h] - g_max[h]) for h in heads]
        kvn = [lax.dot_general(
            k[h], jnp.concatenate([(w_s[h] * v[h].astype(F32)).astype(BF16), w_s[h].astype(BF16)], axis=1),
            _TN, preferred_element_type=F32) for h in heads]
        slot = c % 2
        for h in heads:
            num_s[slot, h] = num_in[h]
            kvn_s[slot, h] = kvn[h]
            col_s[slot, h, 0] = jnp.broadcast_to(den_in[h], (L, LANES))
            col_s[slot, h, 1] = jnp.broadcast_to(m_in[h], (L, LANES))
            col_s[slot, h, 2] = jnp.broadcast_to(bcol[h], (L, LANES))
            row_s[slot, h, 0:1, :] = g_max[h]
            row_s[slot, h, 1:2, :] = jnp.broadcast_to(b_last[h], (1, LANES))

    def staged(c):
        slot = c % 2
        return [(num_s[slot, h], kvn_s[slot, h], col_s[slot, h, 0], col_s[slot, h, 1], col_s[slot, h, 2],
                 row_s[slot, h, 0:1, :], row_s[slot, h, 1:2, :]) for h in heads]

    def state_part(c, stage):
        rows = pl.ds(pl.multiple_of(c * L, L), L)
        q = [q_ref[head_of[h], seq_of[h], rows, :] for h in heads]
        m_prev = [m_ref[units[h]] for h in heads]
        cmat = [c_ref[units[h]] for h in heads]
        ntile = [nt_s[h] for h in heads]
        q_cn = [jnp.dot(q[h], jnp.concatenate([cmat[h].astype(BF16), ntile[h].astype(BF16)], axis=1),
                        preferred_element_type=F32) for h in heads]
        for h in heads:
            num_in, kvn, den_in, m_in, bcol, g_max, b_last = stage[h]
            inter = bcol + m_prev[h]
            m_t = jnp.maximum(m_in, inter)
            f_in, f_st = jnp.exp(m_in - m_t), jnp.exp(inter - m_t)
            num = f_in * num_in + f_st * q_cn[h][:, :ML_DK]
            den = f_in * den_in + f_st * q_cn[h][:, ML_DK:]
            den = jnp.maximum(jnp.abs(den), jnp.exp(-m_t))
            s, hd = units[h]
            y = _rms(num / den, gain_ref[hd]) * jax.nn.sigmoid(mo_ref[hd, s, rows, :])
            hm_ref[s, rows, hd * ML_DK:(hd + 1) * ML_DK] = y.astype(hm_ref.dtype)
            m_new = jnp.maximum(b_last + m_prev[h], g_max)
            w_c = jnp.exp(b_last + m_prev[h] - m_new)
            w_k = jnp.exp(g_max - m_new)
            c_ref[units[h]] = w_c * cmat[h] + w_k * kvn[:, :ML_DK]
            nt_s[h] = w_c * ntile[h] + w_k * kvn[:, ML_DK:]
            m_ref[units[h]] = m_new

    def body(c, carry):
        state_part(c, staged(c))
        local_part(c + 1)
        return carry

    n_chunks = q_ref.shape[2] // L
    local_part(0)
    lax.fori_loop(0, n_chunks - 1, body, 0, unroll=3 if (n_chunks - 1) % 3 == 0 else 1)
    state_part(n_chunks - 1, staged(n_chunks - 1))
    for h in heads:
        n_ref[units[h]] = jnp.sum(jnp.where(eye, nt_s[h], 0.0), axis=0, keepdims=True)


def _mlstm(mq, mk, mv, mo, gates, gain, c0, n0, m0, chunk, block):
    nh, batch, seq, _ = mq.shape
    nu = block * nh
    per_head = pl.BlockSpec((nh, block, seq, LANES), lambda b: (0, b, 0, 0))
    state = lambda r: pl.BlockSpec((block, nh, r, LANES), lambda b: (b, 0, 0, 0))
    return pl.pallas_call(
        functools.partial(_mlstm_kernel, chunk=chunk),
        grid=(batch // block,),
        in_specs=[per_head, per_head, per_head, per_head,
                  pl.BlockSpec((block, seq, LANES), lambda b: (b, 0, 0)),
                  pl.BlockSpec((nh, 1, LANES), lambda b: (0, 0, 0)),
                  state(ML_DK), state(1), state(1)],
        out_specs=[pl.BlockSpec((block, seq, ML_WIDTH), lambda b: (b, 0, 0)), state(ML_DK), state(1), state(1)],
        out_shape=[jax.ShapeDtypeStruct((batch, seq, ML_WIDTH), BF16),
                   jax.ShapeDtypeStruct((batch, nh, ML_DK, LANES), F32),
                   jax.ShapeDtypeStruct((batch, nh, 1, LANES), F32),
                   jax.ShapeDtypeStruct((batch, nh, 1, LANES), F32)],
        scratch_shapes=[pltpu.VMEM((nu, ML_DK, LANES), F32), pltpu.VMEM((2, nu, chunk, LANES), F32),
                        pltpu.VMEM((2, nu, ML_DK, 2 * LANES), F32), pltpu.VMEM((2, nu, 3, chunk, LANES), F32),
                        pltpu.VMEM((2, nu, 8, LANES), F32)],
        compiler_params=pltpu.CompilerParams(dimension_semantics=("arbitrary",), vmem_limit_bytes=VMEM_LIMIT),
        name="mlstm",
    )(mq, mk, mv, mo, gates, gain, c0, n0, m0)


def _rope_tables(pos):
    half = HEAD_DIM // 2
    inv = ROPE_THETA ** (-jnp.arange(half, dtype=F32) / half)
    ang = pos.astype(F32)[:, None] * inv[None, :]
    cos, sin = jnp.cos(ang), jnp.sin(ang)
    return jnp.tile(jnp.concatenate([cos, cos], axis=-1), (1, 2)), jnp.tile(jnp.concatenate([-sin, sin], axis=-1), (1, 2))


def _layer_weights(l, ffn1_norm, ffn1_w_gate, ffn1_w_up, ffn1_w_down, mix_norm, w_in, q_norm, k_norm, b_igate,
                   b_fgate, ml_out_norm, w_out, ffn2_norm, ffn2_w_gate, ffn2_w_up, ffn2_w_down):
    pad_lanes = lambda a: jnp.pad(a, ((0, 0), (0, LANES - a.shape[1])))
    return {
        "g1": ffn1_norm[l][None, :], "wg1": ffn1_w_gate[l].astype(BF16), "wu1": ffn1_w_up[l].astype(BF16),
        "wd1": ffn1_w_down[l].astype(BF16), "gm": mix_norm[l][None, :],
        "win": w_in[l][:, :MAIN_WIDTH].astype(BF16), "wgate": pad_lanes(w_in[l][:, MAIN_WIDTH:]).astype(BF16),
        "bgate": pad_lanes(jnp.concatenate([b_igate[l], b_fgate[l]])[None, :]),
        "qg": jnp.tile(q_norm[l], 2)[None, :], "kg": jnp.tile(k_norm[l], 2)[None, :],
        "gain": ml_out_norm[l][:, None, :], "wo": w_out[l].astype(BF16),
        "g2": ffn2_norm[l][None, :], "wg2": ffn2_w_gate[l].astype(BF16), "wu2": ffn2_w_up[l].astype(BF16),
        "wd2": ffn2_w_down[l].astype(BF16),
    }


def _layer(xp, xs, w, cache_k, cache_v, c0, n0, m0):
    batch, seq, _ = xp.shape
    bd, T, _ = xs.shape
    n, ns = batch * seq, bd * T
    cos, sin = _rope_tables(jnp.arange(seq, dtype=jnp.int32))
    x1, q, k, v, kt, vt, mq, mk, mv, mo, gates = _front(xp.reshape(n, D_MODEL), cos, sin, w, seq)
    cos, sin = _rope_tables(PAST_LEN + jnp.arange(T, dtype=jnp.int32))
    x1s, qs, ks, vs, _, _, mqs, mks, mvs, mos, gates_s = _front(
        xs.reshape(ns, D_MODEL), jnp.tile(cos, (bd, 1)), jnp.tile(sin, (bd, 1)), w, ns)

    seqs = lambda t: t.reshape(batch, seq, ATT_WIDTH)
    news = lambda t: t.reshape(bd, T, ATT_WIDTH)
    transposed = lambda c: c.transpose(0, 2, 3, 1).reshape(bd, ATT_WIDTH, c.shape[1])
    att, att_s = _attention(seqs(q), seqs(k), seqs(v), news(qs), news(ks), news(vs),
                            transposed(cache_k), transposed(cache_v))

    heads = lambda t: t.reshape(N_ML_HEADS, batch, seq, LANES)
    zeros = lambda r: jnp.zeros((batch, N_ML_HEADS, r, LANES), F32)
    hm, c, nv, m = _mlstm(heads(mq), heads(mk), heads(mv), heads(mo), gates.reshape(batch, seq, LANES), w["gain"],
                          zeros(ML_DK), zeros(1), zeros(1), min(ML_CHUNK, seq), 1)
    tp = 8
    heads_s = lambda t: jnp.pad(t.reshape(N_ML_HEADS, bd, T, LANES), ((0, 0), (0, 0), (0, tp - T), (0, 0)))
    pad_gate = jnp.where(jnp.arange(LANES) < N_ML_HEADS, NEG_BIG, 0.0).astype(F32)
    gates_s = jnp.concatenate([gates_s.reshape(bd, T, LANES), jnp.broadcast_to(pad_gate, (bd, tp - T, LANES))], axis=1)
    hm_s, c_s, nv_s, m_s = _mlstm(heads_s(mqs), heads_s(mks), heads_s(mvs), heads_s(mos), gates_s, w["gain"], c0,
                                  n0[:, :, None, :], jnp.broadcast_to(m0[:, :, None, None], (bd, N_ML_HEADS, 1, LANES)),
                                  tp, ML_SAMPLE_BLOCK if bd % ML_SAMPLE_BLOCK == 0 else 1)

    y = _back(x1, att.reshape(n, ATT_WIDTH), hm.reshape(n, ML_WIDTH), w)
    y_s = _back(x1s, att_s.reshape(ns, ATT_WIDTH), hm_s[:, :T].reshape(ns, ML_WIDTH), w)
    kv = lambda t: t.reshape(batch, N_ATT_HEADS, HEAD_DIM, seq).transpose(0, 3, 1, 2)
    kv_s = lambda t: t.reshape(bd, T, N_ATT_HEADS, HEAD_DIM)
    return (y.reshape(batch, seq, D_MODEL), y_s.reshape(bd, T, D_MODEL),
            (kv(kt), kv(vt), kv_s(ks), kv_s(vs), c, nv[:, :, 0, :], m[:, :, 0, 0],
             c_s, nv_s[:, :, 0, :], m_s[:, :, 0, 0]))


def kernel(x_prompt, x_sample, cache_k_win, cache_v_win, state_C, state_n, state_m, ffn1_norm, ffn1_w_gate,
           ffn1_w_up, ffn1_w_down, mix_norm, w_in, q_norm, k_norm, b_igate, b_fgate, ml_out_norm, w_out,
           ffn2_norm, ffn2_w_gate, ffn2_w_up, ffn2_w_down):
    depth = w_in.shape[0]
    yp, ys = x_prompt, x_sample
    outs = [[] for _ in range(10)]
    for l in range(depth):
        w = _layer_weights(l, ffn1_norm, ffn1_w_gate, ffn1_w_up, ffn1_w_down, mix_norm, w_in, q_norm, k_norm,
                           b_igate, b_fgate, ml_out_norm, w_out, ffn2_norm, ffn2_w_gate, ffn2_w_up, ffn2_w_down)
        yp, ys, states = _layer(yp, ys, w, cache_k_win[l], cache_v_win[l], state_C[l], state_n[l], state_m[l])
        for acc, val in zip(outs, states):
            acc.append(val)
    return (yp, ys) + tuple(jnp.stack(o) for o in outs)
```

```python
import functools

import jax
import jax.numpy as jnp
from jax import lax
from jax.experimental import pallas as pl
from jax.experimental.pallas import tpu as pltpu

F32 = jnp.float32
BF16 = jnp.bfloat16

D_MODEL = 1024
HEAD_DIM = 64
N_ATT_HEADS = 8
ATT_WIDTH = N_ATT_HEADS * HEAD_DIM
N_ML_HEADS = 4
ML_DK = 128
ML_WIDTH = N_ML_HEADS * ML_DK
D_FF = 2816
ML_CHUNK = 128
NORM_EPS = 1e-6
ROPE_THETA = 10000.0
PAST_LEN = 8192
DILATED_PATTERNS = ((128, 1), (512, 4), (2048, 16))
N_BACK = 128
ATT_SCALE = HEAD_DIM ** -0.5
LOG2_E = 1.4426950408889634
Q_SCALE = ATT_SCALE * LOG2_E
MAIN_WIDTH = 3 * ATT_WIDTH + 4 * ML_WIDTH

LANES = 128
N_PAIRS = ATT_WIDTH // LANES
ROW_TILE = 256
FRONT_TILE = 2 * ROW_TILE
BACK_TILE = 4 * ROW_TILE
FF_CHUNK = 256
ML_SAMPLE_BLOCK = 8
ATT_GROUP = 8
V7X_VMEM_BYTES = 64 * 1024 * 1024
VMEM_LIMIT = V7X_VMEM_BYTES - 8 * 1024 * 1024
NEG_BIG = -1e30

_NT = (((1,), (1,)), ((), ()))
_TN = (((0,), (0,)), ((), ()))


def _split3(x):
    hi = x.astype(BF16)
    rest = x - hi.astype(F32)
    mid = rest.astype(BF16)
    return hi, mid, (rest - mid.astype(F32)).astype(BF16)


def _rms(x, g):
    return x * lax.rsqrt(jnp.mean(x * x, axis=-1, keepdims=True) + NORM_EPS) * g


def _swiglu(hn, wg_ref, wu_ref, wd_ref, act_ref, rows=slice(None)):
    for c in range(D_FF // FF_CHUNK):
        sl = slice(c * FF_CHUNK, (c + 1) * FF_CHUNK)
        g = jnp.dot(hn, wg_ref[:, sl], preferred_element_type=F32)
        u = jnp.dot(hn, wu_ref[:, sl], preferred_element_type=F32)
        act_ref[rows, sl] = (g * jax.nn.sigmoid(g) * u).astype(BF16)
    return jnp.dot(act_ref[rows, :], wd_ref[...], preferred_element_type=F32)


def _pair_norm_rope(x, gain, cos, sin, lo):
    sq = x * x
    s_lo = jnp.sum(jnp.where(lo, sq, 0.0), axis=-1, keepdims=True)
    s_hi = jnp.sum(jnp.where(lo, 0.0, sq), axis=-1, keepdims=True)
    ms = jnp.where(lo, s_lo, s_hi) * (1.0 / HEAD_DIM)
    y = x * lax.rsqrt(ms + NORM_EPS) * gain
    lane = lax.broadcasted_iota(jnp.int32, y.shape, 1)
    first_half = (lane % HEAD_DIM) < (HEAD_DIM // 2)
    partner = jnp.where(first_half, pltpu.roll(y, LANES - HEAD_DIM // 2, 1), pltpu.roll(y, HEAD_DIM // 2, 1))
    return y * cos + partner * sin


def _front_kernel(x_ref, cos_ref, sin_ref, g1_ref, wg_ref, wu_ref, wd_ref, gm_ref, win_ref, wgate_ref,
                  bgate_ref, qg_ref, kg_ref,
                  x1_ref, q_ref, k_ref, v_ref, kt_ref, vt_ref, mq_ref, mk_ref, mv_ref, mo_ref, gates_ref,
                  act_ref):
    lane = lax.broadcasted_iota(jnp.int32, (ROW_TILE, LANES), 1)
    lo = lane < HEAD_DIM
    pairs = [slice(p * LANES, (p + 1) * LANES) for p in range(N_PAIRS)]
    for r in range(x_ref.shape[0] // ROW_TILE):
        rows = slice(r * ROW_TILE, (r + 1) * ROW_TILE)
        x = x_ref[rows, :]
        hn = _rms(x, g1_ref[...]).astype(BF16)
        x1 = x + 0.5 * _swiglu(hn, wg_ref, wu_ref, wd_ref, act_ref, rows)
        x1_ref[rows, :] = x1
        h = _rms(x1, gm_ref[...]).astype(BF16)

        def proj(i):
            return jnp.dot(h, win_ref[:, i * ATT_WIDTH:(i + 1) * ATT_WIDTH], preferred_element_type=F32)

        cos, sin = cos_ref[rows, :], sin_ref[rows, :]
        aq, ak, av = proj(0), proj(1), proj(2)
        for sl in pairs:
            q_ref[rows, sl] = _pair_norm_rope(aq[:, sl], qg_ref[...], cos, sin, lo) * Q_SCALE
        ak = jnp.concatenate([_pair_norm_rope(ak[:, sl], kg_ref[...], cos, sin, lo) for sl in pairs], axis=1)
        k_ref[rows, :] = ak
        v_ref[rows, :] = av
        kt_ref[:, rows] = ak.T
        vt_ref[:, rows] = av.T
        mq, mk, mv, mo = proj(3), proj(4), proj(5), proj(6)
        for p in range(N_ML_HEADS):
            sl = slice(p * ML_DK, (p + 1) * ML_DK)
            mq_ref[p, rows, :] = mq[:, sl].astype(BF16)
            mk_ref[p, rows, :] = (mk[:, sl] * (ML_DK ** -0.5)).astype(BF16)
            mv_ref[p, rows, :] = mv[:, sl].astype(BF16)
            mo_ref[p, rows, :] = mo[:, sl]
        zg = jnp.dot(h, wgate_ref[...], preferred_element_type=F32) + bgate_ref[...]
        log_sig = jnp.minimum(zg, 0.0) - jnp.log(1.0 + jnp.exp(-jnp.abs(zg)))
        gates_ref[rows, :] = jnp.where(lane < N_ML_HEADS, zg, log_sig)


def _back_kernel(x1_ref, att_ref, hm_ref, wo_ref, g2_ref, wg_ref, wu_ref, wd_ref, y_ref, act_ref):
    for r in range(x1_ref.shape[0] // ROW_TILE):
        rows = slice(r * ROW_TILE, (r + 1) * ROW_TILE)
        x2 = (x1_ref[rows, :]
              + jnp.dot(att_ref[rows, :], wo_ref[:ATT_WIDTH, :], preferred_element_type=F32)
              + jnp.dot(hm_ref[rows, :], wo_ref[ATT_WIDTH:, :], preferred_element_type=F32))
        hn = _rms(x2, g2_ref[...]).astype(BF16)
        y_ref[rows, :] = x2 + 0.5 * _swiglu(hn, wg_ref, wu_ref, wd_ref, act_ref, rows)


def _const_spec(shape):
    return pl.BlockSpec(shape, lambda i: (0,) * len(shape), pipeline_mode=pl.Buffered(1))


def _dense_params():
    return pltpu.CompilerParams(dimension_semantics=("arbitrary",), vmem_limit_bytes=VMEM_LIMIT)


def _front(x, cos, sin, w, seq):
    n = x.shape[0]
    tm = FRONT_TILE
    blocks = seq // tm
    row = lambda width: pl.BlockSpec((tm, width), lambda i: (i, 0))
    rope = pl.BlockSpec((tm, LANES), lambda i: (i % blocks, 0))
    grouped = lambda g: pl.BlockSpec((g, tm, LANES), lambda i: (0, i, 0))
    gshape = lambda g, dt: jax.ShapeDtypeStruct((g, n, LANES), dt)
    transposed = pl.BlockSpec((None, ATT_WIDTH, tm), lambda i: (i // blocks, 0, i % blocks))
    tshape = jax.ShapeDtypeStruct((n // seq, ATT_WIDTH, seq), F32)
    return pl.pallas_call(
        _front_kernel,
        grid=(n // tm,),
        in_specs=[row(D_MODEL), rope, rope,
                  _const_spec((1, D_MODEL)), _const_spec((D_MODEL, D_FF)), _const_spec((D_MODEL, D_FF)),
                  _const_spec((D_FF, D_MODEL)), _const_spec((1, D_MODEL)), _const_spec((D_MODEL, MAIN_WIDTH)),
                  _const_spec((D_MODEL, LANES)), _const_spec((1, LANES)), _const_spec((1, LANES)),
                  _const_spec((1, LANES))],
        out_specs=[row(D_MODEL), row(ATT_WIDTH), row(ATT_WIDTH), row(ATT_WIDTH), transposed, transposed,
                   grouped(N_ML_HEADS), grouped(N_ML_HEADS), grouped(N_ML_HEADS), grouped(N_ML_HEADS), row(LANES)],
        out_shape=[jax.ShapeDtypeStruct((n, D_MODEL), F32), jax.ShapeDtypeStruct((n, ATT_WIDTH), F32),
                   jax.ShapeDtypeStruct((n, ATT_WIDTH), F32), jax.ShapeDtypeStruct((n, ATT_WIDTH), F32),
                   tshape, tshape, gshape(N_ML_HEADS, BF16), gshape(N_ML_HEADS, BF16), gshape(N_ML_HEADS, BF16),
                   gshape(N_ML_HEADS, F32), jax.ShapeDtypeStruct((n, LANES), F32)],
        scratch_shapes=[pltpu.VMEM((tm, D_FF), BF16)],
        compiler_params=_dense_params(),
        name="front",
    )(x, cos, sin, w["g1"], w["wg1"], w["wu1"], w["wd1"], w["gm"], w["win"], w["wgate"], w["bgate"],
      w["qg"], w["kg"])


def _back(x1, att, hm, w):
    n = x1.shape[0]
    tm = min(BACK_TILE, n)
    assert n % tm == 0 and tm % ROW_TILE == 0
    row = lambda width: pl.BlockSpec((tm, width), lambda i: (i, 0))
    return pl.pallas_call(
        _back_kernel,
        grid=(n // tm,),
        in_specs=[row(D_MODEL), row(ATT_WIDTH), row(ML_WIDTH), _const_spec((D_MODEL, D_MODEL)),
                  _const_spec((1, D_MODEL)), _const_spec((D_MODEL, D_FF)), _const_spec((D_MODEL, D_FF)),
                  _const_spec((D_FF, D_MODEL))],
        out_specs=row(D_MODEL),
        out_shape=jax.ShapeDtypeStruct((n, D_MODEL), F32),
        scratch_shapes=[pltpu.VMEM((tm, D_FF), BF16)],
        compiler_params=_dense_params(),
        name="back",
    )(x1, att, hm, w["wo"], w["g2"], w["wg2"], w["wu2"], w["wd2"])


def _attend_group(jobs, lo):
    zero = jnp.zeros((), BF16)
    one = jnp.ones((), BF16)
    scores, values = [], []
    for q, k, v, _ in jobs:
        lo_k = lax.broadcasted_iota(jnp.int32, k.shape, 1) < HEAD_DIM
        k2 = jnp.concatenate([jnp.where(lo_k, k, zero), jnp.where(lo_k, zero, k)], axis=0)
        scores.append(lax.dot_general(q, k2, _NT, preferred_element_type=F32))
        ones_lo = lo_k.astype(F32).astype(BF16)
        ones_hi = one - ones_lo
        values.append(jnp.concatenate(
            [jnp.concatenate([jnp.where(lo_k, v, zero), ones_lo], axis=1),
             jnp.concatenate([jnp.where(lo_k, zero, v), ones_hi], axis=1)], axis=0))
    masked = []
    for (_, k, _, mask), s in zip(jobs, scores):
        nk = k.shape[0]
        masked.append([jnp.where(mask, s[:, h * nk:(h + 1) * nk].astype(BF16), jnp.asarray(-jnp.inf, BF16))
                       for h in range(2)])
    maxes = [[jnp.max(s, axis=-1, keepdims=True) for s in pair] for pair in masked]
    probs = [jnp.concatenate([jnp.exp2(s - m) for s, m in zip(pair, mx)], axis=1)
             for pair, mx in zip(masked, maxes)]
    sums = [jnp.dot(p, v2, preferred_element_type=F32) for p, v2 in zip(probs, values)]
    return [(r[:, :LANES], jnp.where(lo, mx[0].astype(F32), mx[1].astype(F32)), r[:, LANES:])
            for r, mx in zip(sums, maxes)]


def _attn_prompt_kernel(q_ref, k_ref, v_ref, o_ref, acc4_s, m4_s, den4_s, acc16_s, m16_s, den16_s):
    nb = N_BACK
    seq = q_ref.shape[0]
    lo = lax.broadcasted_iota(jnp.int32, (nb, LANES), 1) < HEAD_DIM
    r2 = lax.broadcasted_iota(jnp.int32, (nb, 2 * nb), 0)
    c2 = lax.broadcasted_iota(jnp.int32, (nb, 2 * nb), 1)
    band = (c2 >= r2) & (c2 <= r2 + nb)
    causal = (lax.broadcasted_iota(jnp.int32, (nb, nb), 1)
              <= lax.broadcasted_iota(jnp.int32, (nb, nb), 0))

    def group_jobs(dil, group):
        loaded = {}

        def load(ref, r, blk):
            key = (id(ref), r, blk)
            if key not in loaded:
                rows = pl.ds(r + blk * nb * dil, nb, stride=dil) if dil > 1 else pl.ds(blk * nb, nb)
                loaded[key] = ref[rows, :].astype(BF16)
            return loaded[key]

        jobs = []
        for r, n in group:
            q = load(q_ref, r, n)
            if n == 0:
                jobs.append((q, load(k_ref, r, 0), load(v_ref, r, 0), causal))
            else:
                jobs.append((q, jnp.concatenate([load(k_ref, r, n - 1), load(k_ref, r, n)], axis=0),
                             jnp.concatenate([load(v_ref, r, n - 1), load(v_ref, r, n)], axis=0), band))
        return jobs

    def dilated(dil, acc_s, m_s, den_s):
        blocks = [(r, n) for r in range(dil) for n in range(seq // (nb * dil))]
        for g in range(0, len(blocks), ATT_GROUP):
            group = blocks[g:g + ATT_GROUP]
            for (r, n), (acc, m, den) in zip(group, _attend_group(group_jobs(dil, group), lo)):
                rows = pl.ds(r + n * nb * dil, nb, stride=dil)
                acc_s[rows, :] = acc
                m_s[rows, :] = m
                den_s[rows, :] = den

    dilated(DILATED_PATTERNS[2][1], acc16_s, m16_s, den16_s)
    dilated(DILATED_PATTERNS[1][1], acc4_s, m4_s, den4_s)

    for g in range(0, seq // nb, ATT_GROUP):
        group = list(range(g, g + ATT_GROUP))
        for n, (acc1, m1, den1) in zip(group, _attend_group(group_jobs(1, [(0, n) for n in group]), lo)):
            rows = pl.ds(n * nb, nb)
            m4, m16 = m4_s[rows, :], m16_s[rows, :]
            m = jnp.maximum(jnp.maximum(m1, m4), m16)
            w1, w4, w16 = jnp.exp2(m1 - m), jnp.exp2(m4 - m), jnp.exp2(m16 - m)
            num = w1 * acc1 + w4 * acc4_s[rows, :] + w16 * acc16_s[rows, :]
            den = w1 * den1 + w4 * den4_s[rows, :] + w16 * den16_s[rows, :]
            o_ref[rows, :] = (num / den).astype(o_ref.dtype)


def _attn_sample_kernel(q_ref, kn_ref, vn_ref, kt_ref, vt_ref, o_ref):
    T = q_ref.shape[0]
    lb = kt_ref.shape[1]
    (_, _), (w4, d4), (_, d16) = DILATED_PATTERNS
    na = w4
    q = q_ref[...]
    kn, vn = kn_ref[...], vn_ref[...]
    lane8 = lax.broadcasted_iota(jnp.int32, (N_ATT_HEADS, ATT_WIDTH), 1)
    head8 = lax.broadcasted_iota(jnp.int32, (N_ATT_HEADS, ATT_WIDTH), 0)
    own = (lane8 // HEAD_DIM) == head8
    qrows = jnp.concatenate(
        [jnp.where(own, jnp.broadcast_to(q[t:t + 1, :], own.shape), 0.0) for t in range(T)], axis=0)
    nr = qrows.shape[0]
    qb = qrows.astype(BF16)
    trow = lax.broadcasted_iota(jnp.int32, (nr, 1), 0) // N_ATT_HEADS
    s_all = jnp.dot(qb, kt_ref[...].astype(BF16), preferred_element_type=F32)
    s_a = s_all[:, lb - na:]
    vt = vt_ref[...].astype(BF16)
    knb = kn.astype(BF16).astype(F32)
    s_new = [jnp.sum(qb.astype(F32) * knb[t:t + 1, :], axis=-1, keepdims=True) for t in range(T)]
    col = lax.broadcasted_iota(jnp.int32, (nr, na), 1)
    trow_b = jnp.broadcast_to(trow, (nr, na))
    vnb = vn.astype(BF16).astype(F32)

    def partial_softmax(s_cache, new_ok):
        m = jnp.max(s_cache, axis=-1, keepdims=True)
        for t in range(T):
            m = jnp.maximum(m, jnp.where(new_ok(t), s_new[t], -jnp.inf))
        p = jnp.exp2(s_cache - m)
        den = jnp.sum(p, axis=-1, keepdims=True)
        p_new = [jnp.where(new_ok(t), jnp.exp2(s_new[t] - m), 0.0) for t in range(T)]
        return p.astype(BF16), p_new, m, den + sum(p_new)

    pos = lax.broadcasted_iota(jnp.int32, (nr, lb), 1)
    patterns = [
        partial_softmax(jnp.where(col >= na - N_BACK + trow_b, s_a, -jnp.inf), lambda t: trow >= t),
        partial_softmax(jnp.where((col % d4) == trow_b, s_a, -jnp.inf), lambda t: trow == t),
        partial_softmax(jnp.where((pos % d16) == jnp.broadcast_to(trow, (nr, lb)), s_all, -jnp.inf),
                        lambda t: trow == t),
    ]
    widen = lambda p: p if p.shape[1] == lb else jnp.concatenate([jnp.zeros((nr, lb - na), BF16), p], axis=1)
    p_all = jnp.concatenate([widen(p) for p, _, _, _ in patterns], axis=0)
    acc_all = lax.dot_general(p_all, vt, _NT, preferred_element_type=F32)
    m = jnp.maximum(jnp.maximum(patterns[0][2], patterns[1][2]), patterns[2][2])
    num, den = 0.0, 0.0
    for i, (_, p_new, m_i, den_i) in enumerate(patterns):
        acc = acc_all[i * nr:(i + 1) * nr]
        for t in range(T):
            acc = acc + p_new[t].astype(BF16).astype(F32) * vnb[t:t + 1, :]
        w = jnp.exp2(m_i - m)
        num, den = num + w * acc, den + w * den_i
    merged = num / den
    for t in range(T):
        rows = slice(t * N_ATT_HEADS, (t + 1) * N_ATT_HEADS)
        o_ref[t:t + 1, :] = jnp.sum(jnp.where(own, merged[rows], 0.0), axis=0, keepdims=True).astype(o_ref.dtype)


def _attn_both_kernel(q_ref, k_ref, v_ref, qs_ref, kn_ref, vn_ref, kt_ref, vt_ref, o_ref, os_ref, *scratch):
    _attn_sample_kernel(qs_ref, kn_ref, vn_ref, kt_ref, vt_ref, os_ref)
    _attn_prompt_kernel(q_ref, k_ref, v_ref, o_ref, *scratch)


def _attention(q, k, v, qs, k_new, v_new, cache_kt, cache_vt):
    batch, seq, _ = q.shape
    bd, T, _ = qs.shape
    lb = cache_kt.shape[2]
    (_, _), (w4, d4), (w16, d16) = DILATED_PATTERNS
    for window, dil in DILATED_PATTERNS:
        assert window // dil == N_BACK and seq % (N_BACK * dil) == 0
    assert lb == w16 and lb % d16 == 0 and w4 % d4 == 0 and T <= d4
    params = pltpu.CompilerParams(dimension_semantics=("arbitrary", "arbitrary"), vmem_limit_bytes=VMEM_LIMIT)
    pair = pl.BlockSpec((None, seq, LANES), lambda b, p: (b, 0, p))
    prompt_shape = jax.ShapeDtypeStruct((batch, seq, ATT_WIDTH), BF16)
    prompt_scratch = [pltpu.VMEM((seq, LANES), F32)] * 6

    def sample_specs(index):
        new = pl.BlockSpec((None, T, ATT_WIDTH), lambda b, p: (index(b, p), 0, 0))
        cache = pl.BlockSpec((None, ATT_WIDTH, lb), lambda b, p: (index(b, p), 0, 0))
        return [new, new, new, cache, cache], new

    sample_shape = jax.ShapeDtypeStruct((bd, T, ATT_WIDTH), BF16)
    if bd == batch * N_PAIRS:
        in_s, out_s = sample_specs(lambda b, p: b * N_PAIRS + p)
        return pl.pallas_call(
            _attn_both_kernel, grid=(batch, N_PAIRS), in_specs=[pair, pair, pair] + in_s, out_specs=[pair, out_s],
            out_shape=[prompt_shape, sample_shape], scratch_shapes=prompt_scratch, compiler_params=params,
            name="attention",
        )(q, k, v, qs, k_new, v_new, cache_kt, cache_vt)
    att = pl.pallas_call(
        _attn_prompt_kernel, grid=(batch, N_PAIRS), in_specs=[pair, pair, pair], out_specs=pair,
        out_shape=prompt_shape, scratch_shapes=prompt_scratch, compiler_params=params, name="attn_prompt",
    )(q, k, v)
    in_s, out_s = sample_specs(lambda b, p: b)
    att_s = pl.pallas_call(
        _attn_sample_kernel, grid=(bd, 1), in_specs=in_s, out_specs=out_s, out_shape=sample_shape,
        compiler_params=params, name="attn_sample",
    )(qs, k_new, v_new, cache_kt, cache_vt)
    return att, att_s


def _mlstm_kernel(q_ref, k_ref, v_ref, mo_ref, g_ref, gain_ref, c0_ref, n0_ref, m0_ref,
                  hm_ref, c_ref, n_ref, m_ref, nt_s, num_s, kvn_s, col_s, row_s, *, chunk):
    L = chunk
    seqs = range(g_ref.shape[0])
    units = [(s, hd) for s in seqs for hd in range(N_ML_HEADS)]
    heads = range(len(units))
    seq_of = [s for s, _ in units]
    head_of = [hd for _, hd in units]
    c_ref[...] = c0_ref[...]
    m_ref[...] = m0_ref[...]
    eye = (lax.broadcasted_iota(jnp.int32, (ML_DK, LANES), 0) == lax.broadcasted_iota(jnp.int32, (ML_DK, LANES), 1))
    for h in heads:
        n_col = jnp.sum(jnp.where(eye, jnp.broadcast_to(n0_ref[units[h]], (ML_DK, LANES)), 0.0),
                        axis=-1, keepdims=True)
        nt_s[h] = jnp.broadcast_to(n_col, (ML_DK, LANES))
    lane = lax.broadcasted_iota(jnp.int32, (1, LANES), 1)
    sel_ig = [(lane == head_of[h]).astype(F32) for h in heads]
    sel_lf = [(lane == head_of[h] + N_ML_HEADS).astype(F32) for h in heads]
    row = lax.broadcasted_iota(jnp.int32, (L, L), 0)
    col = lax.broadcasted_iota(jnp.int32, (L, L), 1)
    tril = col <= row
    tril_b = tril.astype(F32).astype(BF16)
    triu_b = (row <= col).astype(F32).astype(BF16)
    eye8_b = (lax.broadcasted_iota(jnp.int32, (8, LANES), 0)
              == lax.broadcasted_iota(jnp.int32, (8, LANES), 1)).astype(F32).astype(BF16)

    def local_part(c):
        rows = pl.ds(pl.multiple_of(c * L, L), L)
        gs = [g_ref[s, rows, :] for s in seqs]
        g_parts = [_split3(gs[s]) for s in seqs]
        b_alls = [sum(jnp.dot(tril_b, p, preferred_element_type=F32) for p in g_parts[s]) for s in seqs]
        if L % LANES == 0:
            g_rows = [gs[s].T[:8] for s in seqs]
            b_rows = [sum(jnp.dot(p, triu_b, preferred_element_type=F32) for p in _split3(g_rows[s])) for s in seqs]
        else:
            g_rows = [sum(lax.dot_general(eye8_b, p, _NT, preferred_element_type=F32) for p in g_parts[s]) for s in seqs]
            b_rows = [sum(lax.dot_general(eye8_b, p, _NT, preferred_element_type=F32) for p in _split3(b_alls[s]))
                      for s in seqs]
        q = [q_ref[head_of[h], seq_of[h], rows, :] for h in heads]
        k = [k_ref[head_of[h], seq_of[h], rows, :] for h in heads]
        v = [v_ref[head_of[h], seq_of[h], rows, :] for h in heads]
        totals = [jnp.sum(gs[s], axis=0, keepdims=True) for s in seqs]
        backs = [totals[s] - b_alls[s] for s in seqs]
        g, b_all, total, back = ([t[seq_of[h]] for h in heads] for t in (gs, b_alls, totals, backs))
        bcol = [jnp.sum(b_all[h] * sel_lf[h], axis=-1, keepdims=True) for h in heads]
        gcol = [jnp.sum(back[h] * sel_lf[h] + g[h] * sel_ig[h], axis=-1, keepdims=True) for h in heads]
        b_last = [jnp.sum(total[h] * sel_lf[h], axis=-1, keepdims=True) for h in heads]
        d = [jnp.where(tril,
                       bcol[h] - b_rows[seq_of[h]][N_ML_HEADS + head_of[h]:N_ML_HEADS + head_of[h] + 1, :]
                       + g_rows[seq_of[h]][head_of[h]:head_of[h] + 1, :], -jnp.inf)
             for h in heads]
        m_in = [jnp.max(d[h], axis=-1, keepdims=True) for h in heads]
        qk = [lax.dot_general(q[h], k[h], _NT, preferred_element_type=F32) for h in heads]
        a = [jnp.exp(d[h] - m_in[h]) * qk[h] for h in heads]
        num_in = [jnp.dot(a[h].astype(BF16), v[h], preferred_element_type=F32) for h in heads]
        den_in = [jnp.sum(a[h], axis=-1, keepdims=True) for h in heads]
        g_tile = [jnp.broadcast_to(gcol[h], (L, LANES)) for h in heads]
        g_max = [jnp.max(g_tile[h], axis=0, keepdims=True) for h in heads]
        w_s = [jnp.exp(g_tile[h] - g_max[h]) for h in heads]
        kvn = [lax.dot_general(
            k[h], jnp.concatenate([(w_s[h] * v[h].astype(F32)).astype(BF16), w_s[h].astype(BF16)], axis=1),
            _TN, preferred_element_type=F32) for h in heads]
        slot = c % 2
        for h in heads:
            num_s[slot, h] = num_in[h]
            kvn_s[slot, h] = kvn[h]
            col_s[slot, h, 0] = jnp.broadcast_to(den_in[h], (L, LANES))
            col_s[slot, h, 1] = jnp.broadcast_to(m_in[h], (L, LANES))
            col_s[slot, h, 2] = jnp.broadcast_to(bcol[h], (L, LANES))
            row_s[slot, h, 0:1, :] = g_max[h]
            row_s[slot, h, 1:2, :] = jnp.broadcast_to(b_last[h], (1, LANES))

    def staged(c):
        slot = c % 2
        return [(num_s[slot, h], kvn_s[slot, h], col_s[slot, h, 0], col_s[slot, h, 1], col_s[slot, h, 2],
                 row_s[slot, h, 0:1, :], row_s[slot, h, 1:2, :]) for h in heads]

    def state_part(c, stage):
        rows = pl.ds(pl.multiple_of(c * L, L), L)
        q = [q_ref[head_of[h], seq_of[h], rows, :] for h in heads]
        m_prev = [m_ref[units[h]] for h in heads]
        cmat = [c_ref[units[h]] for h in heads]
        ntile = [nt_s[h] for h in heads]
        q_cn = [jnp.dot(q[h], jnp.concatenate([cmat[h].astype(BF16), ntile[h].astype(BF16)], axis=1),
                        preferred_element_type=F32) for h in heads]
        for h in heads:
            num_in, kvn, den_in, m_in, bcol, g_max, b_last = stage[h]
            inter = bcol + m_prev[h]
            m_t = jnp.maximum(m_in, inter)
            f_in, f_st = jnp.exp(m_in - m_t), jnp.exp(inter - m_t)
            num = f_in * num_in + f_st * q_cn[h][:, :ML_DK]
            den = f_in * den_in + f_st * q_cn[h][:, ML_DK:]
            den = jnp.maximum(jnp.abs(den), jnp.exp(-m_t))
            s, hd = units[h]
            y = _rms(num / den, gain_ref[hd]) * jax.nn.sigmoid(mo_ref[hd, s, rows, :])
            hm_ref[s, rows, hd * ML_DK:(hd + 1) * ML_DK] = y.astype(hm_ref.dtype)
            m_new = jnp.maximum(b_last + m_prev[h], g_max)
            w_c = jnp.exp(b_last + m_prev[h] - m_new)
            w_k = jnp.exp(g_max - m_new)
            c_ref[units[h]] = w_c * cmat[h] + w_k * kvn[:, :ML_DK]
            nt_s[h] = w_c * ntile[h] + w_k * kvn[:, ML_DK:]
            m_ref[units[h]] = m_new

    def body(c, carry):
        state_part(c, staged(c))
        local_part(c + 1)
        return carry

    n_chunks = q_ref.shape[2] // L
    local_part(0)
    lax.fori_loop(0, n_chunks - 1, body, 0, unroll=3 if (n_chunks - 1) % 3 == 0 else 1)
    state_part(n_chunks - 1, staged(n_chunks - 1))
    for h in heads:
        n_ref[units[h]] = jnp.sum(jnp.where(eye, nt_s[h], 0.0), axis=0, keepdims=True)


def _mlstm(mq, mk, mv, mo, gates, gain, c0, n0, m0, chunk, block):
    nh, batch, seq, _ = mq.shape
    nu = block * nh
    per_head = pl.BlockSpec((nh, block, seq, LANES), lambda b: (0, b, 0, 0))
    state = lambda r: pl.BlockSpec((block, nh, r, LANES), lambda b: (b, 0, 0, 0))
    return pl.pallas_call(
        functools.partial(_mlstm_kernel, chunk=chunk),
        grid=(batch // block,),
        in_specs=[per_head, per_head, per_head, per_head,
                  pl.BlockSpec((block, seq, LANES), lambda b: (b, 0, 0)),
                  pl.BlockSpec((nh, 1, LANES), lambda b: (0, 0, 0)),
                  state(ML_DK), state(1), state(1)],
        out_specs=[pl.BlockSpec((block, seq, ML_WIDTH), lambda b: (b, 0, 0)), state(ML_DK), state(1), state(1)],
        out_shape=[jax.ShapeDtypeStruct((batch, seq, ML_WIDTH), BF16),
                   jax.ShapeDtypeStruct((batch, nh, ML_DK, LANES), F32),
                   jax.ShapeDtypeStruct((batch, nh, 1, LANES), F32),
                   jax.ShapeDtypeStruct((batch, nh, 1, LANES), F32)],
        scratch_shapes=[pltpu.VMEM((nu, ML_DK, LANES), F32), pltpu.VMEM((2, nu, chunk, LANES), F32),
                        pltpu.VMEM((2, nu, ML_DK, 2 * LANES), F32), pltpu.VMEM((2, nu, 3, chunk, LANES), F32),
                        pltpu.VMEM((2, nu, 8, LANES), F32)],
        compiler_params=pltpu.CompilerParams(dimension_semantics=("arbitrary",), vmem_limit_bytes=VMEM_LIMIT),
        name="mlstm",
    )(mq, mk, mv, mo, gates, gain, c0, n0, m0)


def _rope_tables(pos):
    half = HEAD_DIM // 2
    inv = ROPE_THETA ** (-jnp.arange(half, dtype=F32) / half)
    ang = pos.astype(F32)[:, None] * inv[None, :]
    cos, sin = jnp.cos(ang), jnp.sin(ang)
    return jnp.tile(jnp.concatenate([cos, cos], axis=-1), (1, 2)), jnp.tile(jnp.concatenate([-sin, sin], axis=-1), (1, 2))


def _layer_weights(l, ffn1_norm, ffn1_w_gate, ffn1_w_up, ffn1_w_down, mix_norm, w_in, q_norm, k_norm, b_igate,
                   b_fgate, ml_out_norm, w_out, ffn2_norm, ffn2_w_gate, ffn2_w_up, ffn2_w_down):
    pad_lanes = lambda a: jnp.pad(a, ((0, 0), (0, LANES - a.shape[1])))
    return {
        "g1": ffn1_norm[l][None, :], "wg1": ffn1_w_gate[l].astype(BF16), "wu1": ffn1_w_up[l].astype(BF16),
        "wd1": ffn1_w_down[l].astype(BF16), "gm": mix_norm[l][None, :],
        "win": w_in[l][:, :MAIN_WIDTH].astype(BF16), "wgate": pad_lanes(w_in[l][:, MAIN_WIDTH:]).astype(BF16),
        "bgate": pad_lanes(jnp.concatenate([b_igate[l], b_fgate[l]])[None, :]),
        "qg": jnp.tile(q_norm[l], 2)[None, :], "kg": jnp.tile(k_norm[l], 2)[None, :],
        "gain": ml_out_norm[l][:, None, :], "wo": w_out[l].astype(BF16),
        "g2": ffn2_norm[l][None, :], "wg2": ffn2_w_gate[l].astype(BF16), "wu2": ffn2_w_up[l].astype(BF16),
        "wd2": ffn2_w_down[l].astype(BF16),
    }


def _layer(xp, xs, w, cache_k, cache_v, c0, n0, m0):
    batch, seq, _ = xp.shape
    bd, T, _ = xs.shape
    n, ns = batch * seq, bd * T
    cos, sin = _rope_tables(jnp.arange(seq, dtype=jnp.int32))
    x1, q, k, v, kt, vt, mq, mk, mv, mo, gates = _front(xp.reshape(n, D_MODEL), cos, sin, w, seq)
    cos, sin = _rope_tables(PAST_LEN + jnp.arange(T, dtype=jnp.int32))
    x1s, qs, ks, vs, _, _, mqs, mks, mvs, mos, gates_s = _front(
        xs.reshape(ns, D_MODEL), jnp.tile(cos, (bd, 1)), jnp.tile(sin, (bd, 1)), w, ns)

    seqs = lambda t: t.reshape(batch, seq, ATT_WIDTH)
    news = lambda t: t.reshape(bd, T, ATT_WIDTH)
    transposed = lambda c: c.transpose(0, 2, 3, 1).reshape(bd, ATT_WIDTH, c.shape[1])
    att, att_s = _attention(seqs(q), seqs(k), seqs(v), news(qs), news(ks), news(vs),
                            transposed(cache_k), transposed(cache_v))

    heads = lambda t: t.reshape(N_ML_HEADS, batch, seq, LANES)
    zeros = lambda r: jnp.zeros((batch, N_ML_HEADS, r, LANES), F32)
    hm, c, nv, m = _mlstm(heads(mq), heads(mk), heads(mv), heads(mo), gates.reshape(batch, seq, LANES), w["gain"],
                          zeros(ML_DK), zeros(1), zeros(1), min(ML_CHUNK, seq), 1)
    tp = 8
    heads_s = lambda t: jnp.pad(t.reshape(N_ML_HEADS, bd, T, LANES), ((0, 0), (0, 0), (0, tp - T), (0, 0)))
    pad_gate = jnp.where(jnp.arange(LANES) < N_ML_HEADS, NEG_BIG, 0.0).astype(F32)
    gates_s = jnp.concatenate([gates_s.reshape(bd, T, LANES), jnp.broadcast_to(pad_gate, (bd, tp - T, LANES))], axis=1)
    hm_s, c_s, nv_s, m_s = _mlstm(heads_s(mqs), heads_s(mks), heads_s(mvs), heads_s(mos), gates_s, w["gain"], c0,
                                  n0[:, :, None, :], jnp.broadcast_to(m0[:, :, None, None], (bd, N_ML_HEADS, 1, LANES)),
                                  tp, ML_SAMPLE_BLOCK if bd % ML_SAMPLE_BLOCK == 0 else 1)

    y = _back(x1, att.reshape(n, ATT_WIDTH), hm.reshape(n, ML_WIDTH), w)
    y_s = _back(x1s, att_s.reshape(ns, ATT_WIDTH), hm_s[:, :T].reshape(ns, ML_WIDTH), w)
    kv = lambda t: t.reshape(batch, N_ATT_HEADS, HEAD_DIM, seq).transpose(0, 3, 1, 2)
    kv_s = lambda t: t.reshape(bd, T, N_ATT_HEADS, HEAD_DIM)
    return (y.reshape(batch, seq, D_MODEL), y_s.reshape(bd, T, D_MODEL),
            (kv(kt), kv(vt), kv_s(ks), kv_s(vs), c, nv[:, :, 0, :], m[:, :, 0, 0],
             c_s, nv_s[:, :, 0, :], m_s[:, :, 0, 0]))


def kernel(x_prompt, x_sample, cache_k_win, cache_v_win, state_C, state_n, state_m, ffn1_norm, ffn1_w_gate,
           ffn1_w_up, ffn1_w_down, mix_norm, w_in, q_norm, k_norm, b_igate, b_fgate, ml_out_norm, w_out,
           ffn2_norm, ffn2_w_gate, ffn2_w_up, ffn2_w_down):
    depth = w_in.shape[0]
    yp, ys = x_prompt, x_sample
    outs = [[] for _ in range(10)]
    for l in range(depth):
        w = _layer_weights(l, ffn1_norm, ffn1_w_gate, ffn1_w_up, ffn1_w_down, mix_norm, w_in, q_norm, k_norm,
                           b_igate, b_fgate, ml_out_norm, w_out, ffn2_norm, ffn2_w_gate, ffn2_w_up, ffn2_w_down)
        yp, ys, states = _layer(yp, ys, w, cache_k_win[l], cache_v_win[l], state_C[l], state_n[l], state_m[l])
        for acc, val in zip(outs, states):
            acc.append(val)
    return (yp, ys) + tuple(jnp.stack(o) for o in outs)
```

```python
import functools

import jax
import jax.numpy as jnp
from jax import lax
from jax.experimental import pallas as pl
from jax.experimental.pallas import tpu as pltpu

F32 = jnp.float32
BF16 = jnp.bfloat16

D_MODEL = 1024
HEAD_DIM = 64
N_ATT_HEADS = 8
ATT_WIDTH = N_ATT_HEADS * HEAD_DIM
N_ML_HEADS = 4
ML_DK = 128
ML_WIDTH = N_ML_HEADS * ML_DK
D_FF = 2816
ML_CHUNK = 128
NORM_EPS = 1e-6
ROPE_THETA = 10000.0
PAST_LEN = 8192
DILATED_PATTERNS = ((128, 1), (512, 4), (2048, 16))
N_BACK = 128
ATT_SCALE = HEAD_DIM ** -0.5
LOG2_E = 1.4426950408889634
Q_SCALE = ATT_SCALE * LOG2_E
MAIN_WIDTH = 3 * ATT_WIDTH + 4 * ML_WIDTH

LANES = 128
N_PAIRS = ATT_WIDTH // LANES
ROW_TILE = 256
FRONT_TILE = 2 * ROW_TILE
BACK_TILE = 4 * ROW_TILE
FF_CHUNK = 256
ML_SAMPLE_BLOCK = 8
ATT_GROUP = 8
V7X_VMEM_BYTES = 64 * 1024 * 1024
VMEM_LIMIT = V7X_VMEM_BYTES - 8 * 1024 * 1024

_NT = (((1,), (1,)), ((), ()))
_TN = (((0,), (0,)), ((), ()))


def _split3(x):
    hi = x.astype(BF16)
    rest = x - hi.astype(F32)
    mid = rest.astype(BF16)
    return hi, mid, (rest - mid.astype(F32)).astype(BF16)


def _rms(x, g):
    return x * lax.rsqrt(jnp.mean(x * x, axis=-1, keepdims=True) + NORM_EPS) * g


def _swiglu(hn, wg_ref, wu_ref, wd_ref, act_ref, rows=slice(None)):
    for c in range(D_FF // FF_CHUNK):
        sl = slice(c * FF_CHUNK, (c + 1) * FF_CHUNK)
        g = jnp.dot(hn, wg_ref[:, sl], preferred_element_type=F32)
        u = jnp.dot(hn, wu_ref[:, sl], preferred_element_type=F32)
        act_ref[rows, sl] = (g * jax.nn.sigmoid(g) * u).astype(BF16)
    return jnp.dot(act_ref[rows, :], wd_ref[...], preferred_element_type=F32)


def _pair_norm_rope(x, gain, cos, sin, lo):
    sq = x * x
    s_lo = jnp.sum(jnp.where(lo, sq, 0.0), axis=-1, keepdims=True)
    s_hi = jnp.sum(jnp.where(lo, 0.0, sq), axis=-1, keepdims=True)
    ms = jnp.where(lo, s_lo, s_hi) * (1.0 / HEAD_DIM)
    y = x * lax.rsqrt(ms + NORM_EPS) * gain
    lane = lax.broadcasted_iota(jnp.int32, y.shape, 1)
    first_half = (lane % HEAD_DIM) < (HEAD_DIM // 2)
    partner = jnp.where(first_half, pltpu.roll(y, LANES - HEAD_DIM // 2, 1), pltpu.roll(y, HEAD_DIM // 2, 1))
    return y * cos + partner * sin


def _front_kernel(x_ref, cos_ref, sin_ref, g1_ref, wg_ref, wu_ref, wd_ref, gm_ref, win_ref, wgate_ref,
                  bgate_ref, qg_ref, kg_ref,
                  x1_ref, q_ref, k_ref, v_ref, kt_ref, vt_ref, mq_ref, mk_ref, mv_ref, mo_ref, gates_ref,
                  act_ref):
    lane = lax.broadcasted_iota(jnp.int32, (ROW_TILE, LANES), 1)
    lo = lane < HEAD_DIM
    pairs = [slice(p * LANES, (p + 1) * LANES) for p in range(N_PAIRS)]
    for r in range(x_ref.shape[0] // ROW_TILE):
        rows = slice(r * ROW_TILE, (r + 1) * ROW_TILE)
        x = x_ref[rows, :]
        hn = _rms(x, g1_ref[...]).astype(BF16)
        x1 = x + 0.5 * _swiglu(hn, wg_ref, wu_ref, wd_ref, act_ref, rows)
        x1_ref[rows, :] = x1
        h = _rms(x1, gm_ref[...]).astype(BF16)

        def proj(i):
            return jnp.dot(h, win_ref[:, i * ATT_WIDTH:(i + 1) * ATT_WIDTH], preferred_element_type=F32)

        cos, sin = cos_ref[rows, :], sin_ref[rows, :]
        aq, ak, av = proj(0), proj(1), proj(2)
        for sl in pairs:
            q_ref[rows, sl] = _pair_norm_rope(aq[:, sl], qg_ref[...], cos, sin, lo) * Q_SCALE
        ak = jnp.concatenate([_pair_norm_rope(ak[:, sl], kg_ref[...], cos, sin, lo) for sl in pairs], axis=1)
        k_ref[rows, :] = ak
        v_ref[rows, :] = av
        kt_ref[:, rows] = ak.T
        vt_ref[:, rows] = av.T
        mq, mk, mv, mo = proj(3), proj(4), proj(5), proj(6)
        for p in range(N_ML_HEADS):
            sl = slice(p * ML_DK, (p + 1) * ML_DK)
            mq_ref[p, rows, :] = mq[:, sl].astype(BF16)
            mk_ref[p, rows, :] = (mk[:, sl] * (ML_DK ** -0.5)).astype(BF16)
            mv_ref[p, rows, :] = mv[:, sl].astype(BF16)
            mo_ref[p, rows, :] = mo[:, sl]
        zg = jnp.dot(h, wgate_ref[...], preferred_element_type=F32) + bgate_ref[...]
        log_sig = jnp.minimum(zg, 0.0) - jnp.log(1.0 + jnp.exp(-jnp.abs(zg)))
        gates_ref[rows, :] = jnp.where(lane < N_ML_HEADS, zg, log_sig)


def _back_kernel(x1_ref, att_ref, hm_ref, wo_ref, g2_ref, wg_ref, wu_ref, wd_ref, y_ref, act_ref):
    for r in range(x1_ref.shape[0] // ROW_TILE):
        rows = slice(r * ROW_TILE, (r + 1) * ROW_TILE)
        x2 = (x1_ref[rows, :]
              + jnp.dot(att_ref[rows, :], wo_ref[:ATT_WIDTH, :], preferred_element_type=F32)
              + jnp.dot(hm_ref[rows, :], wo_ref[ATT_WIDTH:, :], preferred_element_type=F32))
        hn = _rms(x2, g2_ref[...]).astype(BF16)
        y_ref[rows, :] = x2 + 0.5 * _swiglu(hn, wg_ref, wu_ref, wd_ref, act_ref, rows)


def _const_spec(shape):
    return pl.BlockSpec(shape, lambda i: (0,) * len(shape), pipeline_mode=pl.Buffered(1))


def _dense_params():
    return pltpu.CompilerParams(dimension_semantics=("arbitrary",), vmem_limit_bytes=VMEM_LIMIT)


def _front(x, cos, sin, w, seq):
    n = x.shape[0]
    tm = FRONT_TILE
    blocks = seq // tm
    row = lambda width: pl.BlockSpec((tm, width), lambda i: (i, 0))
    rope = pl.BlockSpec((tm, LANES), lambda i: (i % blocks, 0))
    grouped = lambda g: pl.BlockSpec((g, tm, LANES), lambda i: (0, i, 0))
    gshape = lambda g, dt: jax.ShapeDtypeStruct((g, n, LANES), dt)
    transposed = pl.BlockSpec((None, ATT_WIDTH, tm), lambda i: (i // blocks, 0, i % blocks))
    tshape = jax.ShapeDtypeStruct((n // seq, ATT_WIDTH, seq), F32)
    return pl.pallas_call(
        _front_kernel,
        grid=(n // tm,),
        in_specs=[row(D_MODEL), rope, rope,
                  _const_spec((1, D_MODEL)), _const_spec((D_MODEL, D_FF)), _const_spec((D_MODEL, D_FF)),
                  _const_spec((D_FF, D_MODEL)), _const_spec((1, D_MODEL)), _const_spec((D_MODEL, MAIN_WIDTH)),
                  _const_spec((D_MODEL, LANES)), _const_spec((1, LANES)), _const_spec((1, LANES)),
                  _const_spec((1, LANES))],
        out_specs=[row(D_MODEL), row(ATT_WIDTH), row(ATT_WIDTH), row(ATT_WIDTH), transposed, transposed,
                   grouped(N_ML_HEADS), grouped(N_ML_HEADS), grouped(N_ML_HEADS), grouped(N_ML_HEADS), row(LANES)],
        out_shape=[jax.ShapeDtypeStruct((n, D_MODEL), F32), jax.ShapeDtypeStruct((n, ATT_WIDTH), F32),
                   jax.ShapeDtypeStruct((n, ATT_WIDTH), F32), jax.ShapeDtypeStruct((n, ATT_WIDTH), F32),
                   tshape, tshape, gshape(N_ML_HEADS, BF16), gshape(N_ML_HEADS, BF16), gshape(N_ML_HEADS, BF16),
                   gshape(N_ML_HEADS, F32), jax.ShapeDtypeStruct((n, LANES), F32)],
        scratch_shapes=[pltpu.VMEM((tm, D_FF), BF16)],
        compiler_params=_dense_params(),
        name="front",
    )(x, cos, sin, w["g1"], w["wg1"], w["wu1"], w["wd1"], w["gm"], w["win"], w["wgate"], w["bgate"],
      w["qg"], w["kg"])


def _back(x1, att, hm, w):
    n = x1.shape[0]
    tm = min(BACK_TILE, n)
    assert n % tm == 0 and tm % ROW_TILE == 0
    row = lambda width: pl.BlockSpec((tm, width), lambda i: (i, 0))
    return pl.pallas_call(
        _back_kernel,
        grid=(n // tm,),
        in_specs=[row(D_MODEL), row(ATT_WIDTH), row(ML_WIDTH), _const_spec((D_MODEL, D_MODEL)),
                  _const_spec((1, D_MODEL)), _const_spec((D_MODEL, D_FF)), _const_spec((D_MODEL, D_FF)),
                  _const_spec((D_FF, D_MODEL))],
        out_specs=row(D_MODEL),
        out_shape=jax.ShapeDtypeStruct((n, D_MODEL), F32),
        scratch_shapes=[pltpu.VMEM((tm, D_FF), BF16)],
        compiler_params=_dense_params(),
        name="back",
    )(x1, att, hm, w["wo"], w["g2"], w["wg2"], w["wu2"], w["wd2"])


def _attend_group(jobs, lo):
    zero = jnp.zeros((), BF16)
    one = jnp.ones((), BF16)
    scores, values = [], []
    for q, k, v, _ in jobs:
        lo_k = lax.broadcasted_iota(jnp.int32, k.shape, 1) < HEAD_DIM
        k2 = jnp.concatenate([jnp.where(lo_k, k, zero), jnp.where(lo_k, zero, k)], axis=0)
        scores.append(lax.dot_general(q, k2, _NT, preferred_element_type=F32))
        ones_lo = lo_k.astype(F32).astype(BF16)
        ones_hi = one - ones_lo
        values.append(jnp.concatenate(
            [jnp.concatenate([jnp.where(lo_k, v, zero), ones_lo], axis=1),
             jnp.concatenate([jnp.where(lo_k, zero, v), ones_hi], axis=1)], axis=0))
    masked = []
    for (_, k, _, mask), s in zip(jobs, scores):
        nk = k.shape[0]
        masked.append([jnp.where(mask, s[:, h * nk:(h + 1) * nk].astype(BF16), jnp.asarray(-jnp.inf, BF16))
                       for h in range(2)])
    maxes = [[jnp.max(s, axis=-1, keepdims=True) for s in pair] for pair in masked]
    probs = [jnp.concatenate([jnp.exp2(s - m) for s, m in zip(pair, mx)], axis=1)
             for pair, mx in zip(masked, maxes)]
    sums = [jnp.dot(p, v2, preferred_element_type=F32) for p, v2 in zip(probs, values)]
    return [(r[:, :LANES], jnp.where(lo, mx[0].astype(F32), mx[1].astype(F32)), r[:, LANES:])
            for r, mx in zip(sums, maxes)]


def _attn_prompt_kernel(q_ref, k_ref, v_ref, o_ref, acc4_s, m4_s, den4_s, acc16_s, m16_s, den16_s):
    nb = N_BACK
    seq = q_ref.shape[0]
    lo = lax.broadcasted_iota(jnp.int32, (nb, LANES), 1) < HEAD_DIM
    r2 = lax.broadcasted_iota(jnp.int32, (nb, 2 * nb), 0)
    c2 = lax.broadcasted_iota(jnp.int32, (nb, 2 * nb), 1)
    band = (c2 >= r2) & (c2 <= r2 + nb)
    causal = (lax.broadcasted_iota(jnp.int32, (nb, nb), 1)
              <= lax.broadcasted_iota(jnp.int32, (nb, nb), 0))

    def group_jobs(dil, group):
        loaded = {}

        def load(ref, r, blk):
            key = (id(ref), r, blk)
            if key not in loaded:
                rows = pl.ds(r + blk * nb * dil, nb, stride=dil) if dil > 1 else pl.ds(blk * nb, nb)
                loaded[key] = ref[rows, :].astype(BF16)
            return loaded[key]

        jobs = []
        for r, n in group:
            q = load(q_ref, r, n)
            if n == 0:
                jobs.append((q, load(k_ref, r, 0), load(v_ref, r, 0), causal))
            else:
                jobs.append((q, jnp.concatenate([load(k_ref, r, n - 1), load(k_ref, r, n)], axis=0),
                             jnp.concatenate([load(v_ref, r, n - 1), load(v_ref, r, n)], axis=0), band))
        return jobs

    def dilated(dil, acc_s, m_s, den_s):
        blocks = [(r, n) for r in range(dil) for n in range(seq // (nb * dil))]
        for g in range(0, len(blocks), ATT_GROUP):
            group = blocks[g:g + ATT_GROUP]
            for (r, n), (acc, m, den) in zip(group, _attend_group(group_jobs(dil, group), lo)):
                rows = pl.ds(r + n * nb * dil, nb, stride=dil)
                acc_s[rows, :] = acc
                m_s[rows, :] = m
                den_s[rows, :] = den

    dilated(DILATED_PATTERNS[2][1], acc16_s, m16_s, den16_s)
    dilated(DILATED_PATTERNS[1][1], acc4_s, m4_s, den4_s)

    for g in range(0, seq // nb, ATT_GROUP):
        group = list(range(g, g + ATT_GROUP))
        for n, (acc1, m1, den1) in zip(group, _attend_group(group_jobs(1, [(0, n) for n in group]), lo)):
            rows = pl.ds(n * nb, nb)
            m4, m16 = m4_s[rows, :], m16_s[rows, :]
            m = jnp.maximum(jnp.maximum(m1, m4), m16)
            w1, w4, w16 = jnp.exp2(m1 - m), jnp.exp2(m4 - m), jnp.exp2(m16 - m)
            num = w1 * acc1 + w4 * acc4_s[rows, :] + w16 * acc16_s[rows, :]
            den = w1 * den1 + w4 * den4_s[rows, :] + w16 * den16_s[rows, :]
            o_ref[rows, :] = (num / den).astype(o_ref.dtype)


def _attn_sample_kernel(q_ref, kn_ref, vn_ref, kt_ref, vt_ref, o_ref):
    T = q_ref.shape[0]
    lb = kt_ref.shape[1]
    (_, _), (w4, d4), (_, d16) = DILATED_PATTERNS
    na = w4
    q = q_ref[...]
    kn, vn = kn_ref[...], vn_ref[...]
    lane8 = lax.broadcasted_iota(jnp.int32, (N_ATT_HEADS, ATT_WIDTH), 1)
    head8 = lax.broadcasted_iota(jnp.int32, (N_ATT_HEADS, ATT_WIDTH), 0)
    own = (lane8 // HEAD_DIM) == head8
    qrows = jnp.concatenate(
        [jnp.where(own, jnp.broadcast_to(q[t:t + 1, :], own.shape), 0.0) for t in range(T)], axis=0)
    nr = qrows.shape[0]
    qb = qrows.astype(BF16)
    trow = lax.broadcasted_iota(jnp.int32, (nr, 1), 0) // N_ATT_HEADS
    s_all = jnp.dot(qb, kt_ref[...].astype(BF16), preferred_element_type=F32)
    s_a = s_all[:, lb - na:]
    vt = vt_ref[...].astype(BF16)
    knb = kn.astype(BF16).astype(F32)
    s_new = [jnp.sum(qb.astype(F32) * knb[t:t + 1, :], axis=-1, keepdims=True) for t in range(T)]
    col = lax.broadcasted_iota(jnp.int32, (nr, na), 1)
    trow_b = jnp.broadcast_to(trow, (nr, na))
    vnb = vn.astype(BF16).astype(F32)

    def partial_softmax(s_cache, new_ok):
        m = jnp.max(s_cache, axis=-1, keepdims=True)
        for t in range(T):
            m = jnp.maximum(m, jnp.where(new_ok(t), s_new[t], -jnp.inf))
        p = jnp.exp2(s_cache - m)
        den = jnp.sum(p, axis=-1, keepdims=True)
        p_new = [jnp.where(new_ok(t), jnp.exp2(s_new[t] - m), 0.0) for t in range(T)]
        return p.astype(BF16), p_new, m, den + sum(p_new)

    pos = lax.broadcasted_iota(jnp.int32, (nr, lb), 1)
    patterns = [
        partial_softmax(jnp.where(col >= na - N_BACK + trow_b, s_a, -jnp.inf), lambda t: trow >= t),
        partial_softmax(jnp.where((col % d4) == trow_b, s_a, -jnp.inf), lambda t: trow == t),
        partial_softmax(jnp.where((pos % d16) == jnp.broadcast_to(trow, (nr, lb)), s_all, -jnp.inf),
                        lambda t: trow == t),
    ]
    widen = lambda p: p if p.shape[1] == lb else jnp.concatenate([jnp.zeros((nr, lb - na), BF16), p], axis=1)
    p_all = jnp.concatenate([widen(p) for p, _, _, _ in patterns], axis=0)
    acc_all = lax.dot_general(p_all, vt, _NT, preferred_element_type=F32)
    m = jnp.maximum(jnp.maximum(patterns[0][2], patterns[1][2]), patterns[2][2])
    num, den = 0.0, 0.0
    for i, (_, p_new, m_i, den_i) in enumerate(patterns):
        acc = acc_all[i * nr:(i + 1) * nr]
        for t in range(T):
            acc = acc + p_new[t].astype(BF16).astype(F32) * vnb[t:t + 1, :]
        w = jnp.exp2(m_i - m)
        num, den = num + w * acc, den + w * den_i
    merged = num / den
    for t in range(T):
        rows = slice(t * N_ATT_HEADS, (t + 1) * N_ATT_HEADS)
        o_ref[t:t + 1, :] = jnp.sum(jnp.where(own, merged[rows], 0.0), axis=0, keepdims=True).astype(o_ref.dtype)


def _attn_both_kernel(q_ref, k_ref, v_ref, qs_ref, kn_ref, vn_ref, kt_ref, vt_ref, o_ref, os_ref, *scratch):
    _attn_sample_kernel(qs_ref, kn_ref, vn_ref, kt_ref, vt_ref, os_ref)
    _attn_prompt_kernel(q_ref, k_ref, v_ref, o_ref, *scratch)


def _attention(q, k, v, qs, k_new, v_new, cache_kt, cache_vt):
    batch, seq, _ = q.shape
    bd, T, _ = qs.shape
    lb = cache_kt.shape[2]
    (_, _), (w4, d4), (w16, d16) = DILATED_PATTERNS
    for window, dil in DILATED_PATTERNS:
        assert window // dil == N_BACK and seq % (N_BACK * dil) == 0
    assert lb == w16 and lb % d16 == 0 and w4 % d4 == 0 and T <= d4
    params = pltpu.CompilerParams(dimension_semantics=("arbitrary", "arbitrary"), vmem_limit_bytes=VMEM_LIMIT)
    pair = pl.BlockSpec((None, seq, LANES), lambda b, p: (b, 0, p))
    prompt_shape = jax.ShapeDtypeStruct((batch, seq, ATT_WIDTH), BF16)
    prompt_scratch = [pltpu.VMEM((seq, LANES), F32)] * 6

    def sample_specs(index):
        new = pl.BlockSpec((None, T, ATT_WIDTH), lambda b, p: (index(b, p), 0, 0))
        cache = pl.BlockSpec((None, ATT_WIDTH, lb), lambda b, p: (index(b, p), 0, 0))
        return [new, new, new, cache, cache], new

    sample_shape = jax.ShapeDtypeStruct((bd, T, ATT_WIDTH), BF16)
    if bd == batch * N_PAIRS:
        in_s, out_s = sample_specs(lambda b, p: b * N_PAIRS + p)
        return pl.pallas_call(
            _attn_both_kernel, grid=(batch, N_PAIRS), in_specs=[pair, pair, pair] + in_s, out_specs=[pair, out_s],
            out_shape=[prompt_shape, sample_shape], scratch_shapes=prompt_scratch, compiler_params=params,
            name="attention",
        )(q, k, v, qs, k_new, v_new, cache_kt, cache_vt)
    att = pl.pallas_call(
        _attn_prompt_kernel, grid=(batch, N_PAIRS), in_specs=[pair, pair, pair], out_specs=pair,
        out_shape=prompt_shape, scratch_shapes=prompt_scratch, compiler_params=params, name="attn_prompt",
    )(q, k, v)
    in_s, out_s = sample_specs(lambda b, p: b)
    att_s = pl.pallas_call(
        _attn_sample_kernel, grid=(bd, 1), in_specs=in_s, out_specs=out_s, out_shape=sample_shape,
        compiler_params=params, name="attn_sample",
    )(qs, k_new, v_new, cache_kt, cache_vt)
    return att, att_s


def _mlstm_kernel(q_ref, k_ref, v_ref, mo_ref, g_ref, gain_ref, c0_ref, n0_ref, m0_ref,
                  hm_ref, c_ref, n_ref, m_ref, nt_s, num_s, kvn_s, col_s, row_s, *, chunk, valid):
    L = chunk
    seqs = range(g_ref.shape[0])
    units = [(s, hd) for s in seqs for hd in range(N_ML_HEADS)]
    heads = range(len(units))
    seq_of = [s for s, _ in units]
    head_of = [hd for _, hd in units]
    c_ref[...] = c0_ref[...]
    m_ref[...] = m0_ref[...]
    eye = (lax.broadcasted_iota(jnp.int32, (ML_DK, LANES), 0) == lax.broadcasted_iota(jnp.int32, (ML_DK, LANES), 1))
    for h in heads:
        n_col = jnp.sum(jnp.where(eye, jnp.broadcast_to(n0_ref[units[h]], (ML_DK, LANES)), 0.0),
                        axis=-1, keepdims=True)
        nt_s[h] = jnp.broadcast_to(n_col, (ML_DK, LANES))
    lane = lax.broadcasted_iota(jnp.int32, (1, LANES), 1)
    sel_ig = [(lane == head_of[h]).astype(F32) for h in heads]
    sel_lf = [(lane == head_of[h] + N_ML_HEADS).astype(F32) for h in heads]
    row = lax.broadcasted_iota(jnp.int32, (L, L), 0)
    col = lax.broadcasted_iota(jnp.int32, (L, L), 1)
    tril = col <= row
    keep = tril if valid == L else tril & (col < valid)
    real_row = None if valid == L else lax.broadcasted_iota(jnp.int32, (L, LANES), 0) < valid
    tril_b = tril.astype(F32).astype(BF16)
    triu_b = (row <= col).astype(F32).astype(BF16)
    eye8_b = (lax.broadcasted_iota(jnp.int32, (8, LANES), 0)
              == lax.broadcasted_iota(jnp.int32, (8, LANES), 1)).astype(F32).astype(BF16)

    def local_part(c):
        rows = pl.ds(pl.multiple_of(c * L, L), L)
        gs = [g_ref[s, rows, :] for s in seqs]
        g_parts = [_split3(gs[s]) for s in seqs]
        b_alls = [sum(jnp.dot(tril_b, p, preferred_element_type=F32) for p in g_parts[s]) for s in seqs]
        if L % LANES == 0:
            g_rows = [gs[s].T[:8] for s in seqs]
            b_rows = [sum(jnp.dot(p, triu_b, preferred_element_type=F32) for p in _split3(g_rows[s])) for s in seqs]
        else:
            g_rows = [sum(lax.dot_general(eye8_b, p, _NT, preferred_element_type=F32) for p in g_parts[s]) for s in seqs]
            b_rows = [sum(lax.dot_general(eye8_b, p, _NT, preferred_element_type=F32) for p in _split3(b_alls[s]))
                      for s in seqs]
        q = [q_ref[head_of[h], seq_of[h], rows, :] for h in heads]
        k = [k_ref[head_of[h], seq_of[h], rows, :] for h in heads]
        v = [v_ref[head_of[h], seq_of[h], rows, :] for h in heads]
        totals = [jnp.sum(gs[s], axis=0, keepdims=True) for s in seqs]
        backs = [totals[s] - b_alls[s] for s in seqs]
        g, b_all, total, back = ([t[seq_of[h]] for h in heads] for t in (gs, b_alls, totals, backs))
        bcol = [jnp.sum(b_all[h] * sel_lf[h], axis=-1, keepdims=True) for h in heads]
        gcol = [jnp.sum(back[h] * sel_lf[h] + g[h] * sel_ig[h], axis=-1, keepdims=True) for h in heads]
        b_last = [jnp.sum(total[h] * sel_lf[h], axis=-1, keepdims=True) for h in heads]
        d = [jnp.where(keep,
                       bcol[h] - b_rows[seq_of[h]][N_ML_HEADS + head_of[h]:N_ML_HEADS + head_of[h] + 1, :]
                       + g_rows[seq_of[h]][head_of[h]:head_of[h] + 1, :], -jnp.inf)
             for h in heads]
        m_in = [jnp.max(d[h], axis=-1, keepdims=True) for h in heads]
        qk = [lax.dot_general(q[h], k[h], _NT, preferred_element_type=F32) for h in heads]
        a = [jnp.exp(d[h] - m_in[h]) * qk[h] for h in heads]
        num_in = [jnp.dot(a[h].astype(BF16), v[h], preferred_element_type=F32) for h in heads]
        den_in = [jnp.sum(a[h], axis=-1, keepdims=True) for h in heads]
        g_tile = [jnp.broadcast_to(gcol[h], (L, LANES)) for h in heads]
        if real_row is not None:
            g_tile = [jnp.where(real_row, t, -jnp.inf) for t in g_tile]
        g_max = [jnp.max(g_tile[h], axis=0, keepdims=True) for h in heads]
        w_s = [jnp.exp(g_tile[h] - g_max[h]) for h in heads]
        kvn = [lax.dot_general(
            k[h], jnp.concatenate([(w_s[h] * v[h].astype(F32)).astype(BF16), w_s[h].astype(BF16)], axis=1),
            _TN, preferred_element_type=F32) for h in heads]
        slot = c % 2
        for h in heads:
            num_s[slot, h] = num_in[h]
            kvn_s[slot, h] = kvn[h]
            col_s[slot, h, 0] = jnp.broadcast_to(den_in[h], (L, LANES))
            col_s[slot, h, 1] = jnp.broadcast_to(m_in[h], (L, LANES))
            col_s[slot, h, 2] = jnp.broadcast_to(bcol[h], (L, LANES))
            row_s[slot, h, 0:1, :] = g_max[h]
            row_s[slot, h, 1:2, :] = jnp.broadcast_to(b_last[h], (1, LANES))

    def staged(c):
        slot = c % 2
        return [(num_s[slot, h], kvn_s[slot, h], col_s[slot, h, 0], col_s[slot, h, 1], col_s[slot, h, 2],
                 row_s[slot, h, 0:1, :], row_s[slot, h, 1:2, :]) for h in heads]

    def state_part(c, stage):
        rows = pl.ds(pl.multiple_of(c * L, L), L)
        q = [q_ref[head_of[h], seq_of[h], rows, :] for h in heads]
        m_prev = [m_ref[units[h]] for h in heads]
        cmat = [c_ref[units[h]] for h in heads]
        ntile = [nt_s[h] for h in heads]
        q_cn = [jnp.dot(q[h], jnp.concatenate([cmat[h].astype(BF16), ntile[h].astype(BF16)], axis=1),
                        preferred_element_type=F32) for h in heads]
        for h in heads:
            num_in, kvn, den_in, m_in, bcol, g_max, b_last = stage[h]
            inter = bcol + m_prev[h]
            m_t = jnp.maximum(m_in, inter)
            f_in, f_st = jnp.exp(m_in - m_t), jnp.exp(inter - m_t)
            num = f_in * num_in + f_st * q_cn[h][:, :ML_DK]
            den = f_in * den_in + f_st * q_cn[h][:, ML_DK:]
            den = jnp.maximum(jnp.abs(den), jnp.exp(-m_t))
            s, hd = units[h]
            y = _rms(num / den, gain_ref[hd]) * jax.nn.sigmoid(mo_ref[hd, s, rows, :])
            hm_ref[s, rows, hd * ML_DK:(hd + 1) * ML_DK] = y.astype(hm_ref.dtype)
            m_new = jnp.maximum(b_last + m_prev[h], g_max)
            w_c = jnp.exp(b_last + m_prev[h] - m_new)
            w_k = jnp.exp(g_max - m_new)
            c_ref[units[h]] = w_c * cmat[h] + w_k * kvn[:, :ML_DK]
            nt_s[h] = w_c * ntile[h] + w_k * kvn[:, ML_DK:]
            m_ref[units[h]] = m_new

    def body(c, carry):
        state_part(c, staged(c))
        local_part(c + 1)
        return carry

    n_chunks = q_ref.shape[2] // L
    local_part(0)
    lax.fori_loop(0, n_chunks - 1, body, 0, unroll=3 if (n_chunks - 1) % 3 == 0 else 1)
    state_part(n_chunks - 1, staged(n_chunks - 1))
    for h in heads:
        n_ref[units[h]] = jnp.sum(jnp.where(eye, nt_s[h], 0.0), axis=0, keepdims=True)


def _mlstm(mq, mk, mv, mo, gates, gain, c0, n0, m0, chunk, block, valid=None):
    nh, batch, seq, _ = mq.shape
    valid = chunk if valid is None else valid
    assert valid == chunk or seq == chunk
    nu = block * nh
    per_head = pl.BlockSpec((nh, block, seq, LANES), lambda b: (0, b, 0, 0))
    state = lambda r: pl.BlockSpec((block, nh, r, LANES), lambda b: (b, 0, 0, 0))
    return pl.pallas_call(
        functools.partial(_mlstm_kernel, chunk=chunk, valid=valid),
        grid=(batch // block,),
        in_specs=[per_head, per_head, per_head, per_head,
                  pl.BlockSpec((block, seq, LANES), lambda b: (b, 0, 0)),
                  pl.BlockSpec((nh, 1, LANES), lambda b: (0, 0, 0)),
                  state(ML_DK), state(1), state(1)],
        out_specs=[pl.BlockSpec((block, seq, ML_WIDTH), lambda b: (b, 0, 0)), state(ML_DK), state(1), state(1)],
        out_shape=[jax.ShapeDtypeStruct((batch, seq, ML_WIDTH), BF16),
                   jax.ShapeDtypeStruct((batch, nh, ML_DK, LANES), F32),
                   jax.ShapeDtypeStruct((batch, nh, 1, LANES), F32),
                   jax.ShapeDtypeStruct((batch, nh, 1, LANES), F32)],
        scratch_shapes=[pltpu.VMEM((nu, ML_DK, LANES), F32), pltpu.VMEM((2, nu, chunk, LANES), F32),
                        pltpu.VMEM((2, nu, ML_DK, 2 * LANES), F32), pltpu.VMEM((2, nu, 3, chunk, LANES), F32),
                        pltpu.VMEM((2, nu, 8, LANES), F32)],
        compiler_params=pltpu.CompilerParams(dimension_semantics=("arbitrary",), vmem_limit_bytes=VMEM_LIMIT),
        name="mlstm",
    )(mq, mk, mv, mo, gates, gain, c0, n0, m0)


def _rope_tables(pos):
    half = HEAD_DIM // 2
    inv = ROPE_THETA ** (-jnp.arange(half, dtype=F32) / half)
    ang = pos.astype(F32)[:, None] * inv[None, :]
    cos, sin = jnp.cos(ang), jnp.sin(ang)
    return jnp.tile(jnp.concatenate([cos, cos], axis=-1), (1, 2)), jnp.tile(jnp.concatenate([-sin, sin], axis=-1), (1, 2))


def _layer_weights(l, ffn1_norm, ffn1_w_gate, ffn1_w_up, ffn1_w_down, mix_norm, w_in, q_norm, k_norm, b_igate,
                   b_fgate, ml_out_norm, w_out, ffn2_norm, ffn2_w_gate, ffn2_w_up, ffn2_w_down):
    pad_lanes = lambda a: jnp.pad(a, ((0, 0), (0, LANES - a.shape[1])))
    return {
        "g1": ffn1_norm[l][None, :], "wg1": ffn1_w_gate[l].astype(BF16), "wu1": ffn1_w_up[l].astype(BF16),
        "wd1": ffn1_w_down[l].astype(BF16), "gm": mix_norm[l][None, :],
        "win": w_in[l][:, :MAIN_WIDTH].astype(BF16), "wgate": pad_lanes(w_in[l][:, MAIN_WIDTH:]).astype(BF16),
        "bgate": pad_lanes(jnp.concatenate([b_igate[l], b_fgate[l]])[None, :]),
        "qg": jnp.tile(q_norm[l], 2)[None, :], "kg": jnp.tile(k_norm[l], 2)[None, :],
        "gain": ml_out_norm[l][:, None, :], "wo": w_out[l].astype(BF16),
        "g2": ffn2_norm[l][None, :], "wg2": ffn2_w_gate[l].astype(BF16), "wu2": ffn2_w_up[l].astype(BF16),
        "wd2": ffn2_w_down[l].astype(BF16),
    }


def _layer(xp, xs, w, cache_k, cache_v, c0, n0, m0):
    batch, seq, _ = xp.shape
    bd, T, _ = xs.shape
    n, ns = batch * seq, bd * T
    cos, sin = _rope_tables(jnp.arange(seq, dtype=jnp.int32))
    x1, q, k, v, kt, vt, mq, mk, mv, mo, gates = _front(xp.reshape(n, D_MODEL), cos, sin, w, seq)
    cos, sin = _rope_tables(PAST_LEN + jnp.arange(T, dtype=jnp.int32))
    x1s, qs, ks, vs, _, _, mqs, mks, mvs, mos, gates_s = _front(
        xs.reshape(ns, D_MODEL), jnp.tile(cos, (bd, 1)), jnp.tile(sin, (bd, 1)), w, ns)

    seqs = lambda t: t.reshape(batch, seq, ATT_WIDTH)
    news = lambda t: t.reshape(bd, T, ATT_WIDTH)
    transposed = lambda c: c.transpose(0, 2, 3, 1).reshape(bd, ATT_WIDTH, c.shape[1])
    att, att_s = _attention(seqs(q), seqs(k), seqs(v), news(qs), news(ks), news(vs),
                            transposed(cache_k), transposed(cache_v))

    heads = lambda t: t.reshape(N_ML_HEADS, batch, seq, LANES)
    zeros = lambda r: jnp.zeros((batch, N_ML_HEADS, r, LANES), F32)
    hm, c, nv, m = _mlstm(heads(mq), heads(mk), heads(mv), heads(mo), gates.reshape(batch, seq, LANES), w["gain"],
                          zeros(ML_DK), zeros(1), zeros(1), min(ML_CHUNK, seq), 1)
    tp = 8
    heads_s = lambda t: jnp.pad(t.reshape(N_ML_HEADS, bd, T, LANES), ((0, 0), (0, 0), (0, tp - T), (0, 0)))
    gates_s = jnp.pad(gates_s.reshape(bd, T, LANES), ((0, 0), (0, tp - T), (0, 0)))
    hm_s, c_s, nv_s, m_s = _mlstm(heads_s(mqs), heads_s(mks), heads_s(mvs), heads_s(mos), gates_s, w["gain"], c0,
                                  n0[:, :, None, :], jnp.broadcast_to(m0[:, :, None, None], (bd, N_ML_HEADS, 1, LANES)),
                                  tp, ML_SAMPLE_BLOCK if bd % ML_SAMPLE_BLOCK == 0 else 1, valid=T)

    y = _back(x1, att.reshape(n, ATT_WIDTH), hm.reshape(n, ML_WIDTH), w)
    y_s = _back(x1s, att_s.reshape(ns, ATT_WIDTH), hm_s[:, :T].reshape(ns, ML_WIDTH), w)
    kv = lambda t: t.reshape(batch, N_ATT_HEADS, HEAD_DIM, seq).transpose(0, 3, 1, 2)
    kv_s = lambda t: t.reshape(bd, T, N_ATT_HEADS, HEAD_DIM)
    return (y.reshape(batch, seq, D_MODEL), y_s.reshape(bd, T, D_MODEL),
            (kv(kt), kv(vt), kv_s(ks), kv_s(vs), c, nv[:, :, 0, :], m[:, :, 0, 0],
             c_s, nv_s[:, :, 0, :], m_s[:, :, 0, 0]))


def kernel(x_prompt, x_sample, cache_k_win, cache_v_win, state_C, state_n, state_m, ffn1_norm, ffn1_w_gate,
           ffn1_w_up, ffn1_w_down, mix_norm, w_in, q_norm, k_norm, b_igate, b_fgate, ml_out_norm, w_out,
           ffn2_norm, ffn2_w_gate, ffn2_w_up, ffn2_w_down):
    depth = w_in.shape[0]
    yp, ys = x_prompt, x_sample
    outs = [[] for _ in range(10)]
    for l in range(depth):
        w = _layer_weights(l, ffn1_norm, ffn1_w_gate, ffn1_w_up, ffn1_w_down, mix_norm, w_in, q_norm, k_norm,
                           b_igate, b_fgate, ml_out_norm, w_out, ffn2_norm, ffn2_w_gate, ffn2_w_up, ffn2_w_down)
        yp, ys, states = _layer(yp, ys, w, cache_k_win[l], cache_v_win[l], state_C[l], state_n[l], state_m[l])
        for acc, val in zip(outs, states):
            acc.append(val)
    return (yp, ys) + tuple(jnp.stack(o) for o in outs)
```
